```python
import math
import jax, jax.numpy as jnp
from jax import lax
import numpy as np

D_MODEL = 1024
BATCH = 4
SEQ = 8192
DEPTH = 1

RG_WIDTH = 1024
RG_BLOCKS = 4
RG_BLOCK_DIM = RG_WIDTH // RG_BLOCKS
RG_C = 8.0
CONV_WIDTH = 4
DN_HEADS = 8
DN_DK = 128
DN_DV = 128
DN_CHUNK = 64
N_GROUPS = 4
EXPERTS_PER_GROUP = 8
N_EXPERTS = N_GROUPS * EXPERTS_PER_GROUP
TOP_K = 2
D_EXPERT = 512
MOE_BLOCK = 128
NORM_EPS = 1e-6
IN_SIZES = (RG_WIDTH, RG_WIDTH, DN_HEADS * DN_DK, DN_HEADS * DN_DK, DN_HEADS * DN_DV, DN_HEADS * DN_DV, DN_HEADS, DN_HEADS, D_MODEL, D_MODEL)
D_IN = sum(IN_SIZES)
DN_CONV_CH = 2 * DN_HEADS * DN_DK + DN_HEADS * DN_DV

kernel_name = "hybrid_rglru_gdn_hmoe_adaln"


def rms_norm(x, w):
    xf = x.astype(jnp.float32)
    y = xf * lax.rsqrt(jnp.mean(xf * xf, axis=-1, keepdims=True) + NORM_EPS)
    return (y * w.astype(jnp.float32)).astype(x.dtype)


def l2_norm(t):
    return t * lax.rsqrt(jnp.sum(t * t, axis=-1, keepdims=True) + NORM_EPS)


def split_cols(t, sizes):
    idx = np.cumsum(sizes)[:-1].tolist()
    return jnp.split(t, idx, axis=-1)


def causal_conv(x, w):
    k_width, ch = w.shape
    return lax.conv_general_dilated(x, w.astype(x.dtype)[:, None, :], window_strides=(1,), padding=[(k_width - 1, 0)], dimension_numbers=('NWC', 'WIO', 'NWC'), feature_group_count=ch)


def rg_lru(xr, w_a, b_a, w_x, b_x, lam):
    B, S, _ = xr.shape
    f32 = jnp.float32
    xf = xr.astype(f32)
    xb = xf.reshape(B, S, RG_BLOCKS, RG_BLOCK_DIM)
    r = jax.nn.sigmoid(jnp.einsum('bsgi,gij->bsgj', xb, w_a.astype(f32)) + b_a.astype(f32)).reshape(B, S, RG_WIDTH)
    i = jax.nn.sigmoid(jnp.einsum('bsgi,gij->bsgj', xb, w_x.astype(f32)) + b_x.astype(f32)).reshape(B, S, RG_WIDTH)
    log_a = -RG_C * r * jax.nn.softplus(-lam.astype(f32))
    a = jnp.exp(log_a)
    b = jnp.sqrt(-jnp.expm1(2.0 * log_a)) * (i * xf)

    def combine(left, right):
        a1, b1 = left
        a2, b2 = right
        return a1 * a2, a2 * b1 + b2

    _, h = lax.associative_scan(combine, (a, b), axis=1)
    return h


def chunk_gated_delta_rule(q, k, v, beta, g):
    B, S, H, DK = q.shape
    DV = v.shape[-1]
    C = DN_CHUNK
    N = S // C

    def chunks(t):
        return jnp.moveaxis(t.reshape(B, N, C, H, -1), 3, 1)

    q, k, v = chunks(q), chunks(k), chunks(v)
    beta = jnp.moveaxis(beta.reshape(B, N, C, H), 3, 1)
    g = jnp.cumsum(jnp.moveaxis(g.reshape(B, N, C, H), 3, 1), axis=-1)
    causal = jnp.tril(jnp.ones((C, C), bool))
    strict = jnp.tril(jnp.ones((C, C), bool), -1)
    decay = jnp.exp(jnp.where(causal, g[..., :, None] - g[..., None, :], -jnp.inf))
    kb = k * beta[..., None]
    a_mat = jnp.where(strict, jnp.einsum('bhnik,bhnjk->bhnij', kb, k) * decay, 0.0)
    lower = a_mat + jnp.eye(C, dtype=a_mat.dtype)
    rhs = jnp.concatenate([v * beta[..., None], kb * jnp.exp(g)[..., None]], axis=-1)
    sol = lax.linalg.triangular_solve(lower, rhs, left_side=True, lower=True, unit_diagonal=True)
    u, w = sol[..., :DV], sol[..., DV:]
    qk = jnp.where(causal, jnp.einsum('bhnik,bhnjk->bhnij', q, k) * decay, 0.0)
    q_dec = q * jnp.exp(g)[..., None]
    k_dec = k * jnp.exp(g[..., -1:] - g)[..., None]
    g_end = jnp.exp(g[..., -1])

    def step(state, inp):
        q_c, k_c, u_c, w_c, qk_c, ge = inp
        v_new = u_c - jnp.einsum('bhck,bhkv->bhcv', w_c, state)
        o_c = jnp.einsum('bhck,bhkv->bhcv', q_c, state) + jnp.einsum('bhij,bhjv->bhiv', qk_c, v_new)
        state = state * ge[..., None, None] + jnp.einsum('bhck,bhcv->bhkv', k_c, v_new)
        return state, o_c

    xs = tuple(jnp.moveaxis(t, 2, 0) for t in (q_dec, k_dec, u, w, qk, g_end))
    state0 = jnp.zeros((B, H, DK, DV), q.dtype)
    _, o = lax.scan(step, state0, xs)
    return o.transpose(1, 0, 3, 2, 4).reshape(B, S, H, DV)


def hybrid_mixer(h, w_in, rg_conv_w, rg_conv_b, rg_gate_a_w, rg_gate_a_b, rg_gate_x_w, rg_gate_x_b, rg_lambda, dn_conv_w, dn_a_log, dn_dt_bias, dn_norm_w, w_branch_rg, w_branch_dn, w_out):
    B, S, _ = h.shape
    f32 = jnp.float32
    proj = h @ w_in
    rg_x, rg_y, q, k, v, z, beta_in, alpha_in, gate_rg, gate_dn = split_cols(proj, IN_SIZES)
    rg_x = causal_conv(rg_x, rg_conv_w) + rg_conv_b
    rg_h = rg_lru(rg_x, rg_gate_a_w, rg_gate_a_b, rg_gate_x_w, rg_gate_x_b, rg_lambda)
    y_rg = (rg_h * jax.nn.gelu(rg_y.astype(f32))).astype(h.dtype)
    qkv = jax.nn.silu(causal_conv(jnp.concatenate([q, k, v], axis=-1), dn_conv_w)).astype(f32)
    q, k, v = split_cols(qkv, (DN_HEADS * DN_DK, DN_HEADS * DN_DK, DN_HEADS * DN_DV))
    q = l2_norm(q.reshape(B, S, DN_HEADS, DN_DK)) * (DN_DK ** -0.5)
    k = l2_norm(k.reshape(B, S, DN_HEADS, DN_DK))
    v = v.reshape(B, S, DN_HEADS, DN_DV)
    beta = jax.nn.sigmoid(beta_in.astype(f32))
    g = -jnp.exp(dn_a_log.astype(f32)) * jax.nn.softplus(alpha_in.astype(f32) + dn_dt_bias.astype(f32))
    o = chunk_gated_delta_rule(q, k, v, beta, g)
    o = rms_norm(o, dn_norm_w) * jax.nn.silu(z.astype(f32).reshape(B, S, DN_HEADS, DN_DV))
    y_dn = o.reshape(B, S, DN_HEADS * DN_DV).astype(h.dtype)
    merged = jax.nn.sigmoid(gate_rg) * (y_rg @ w_branch_rg) + jax.nn.sigmoid(gate_dn) * (y_dn @ w_branch_dn)
    return merged @ w_out


def hier_moe(h, w_group, b_group, w_router, b_router, w_gate, w_up, w_down):
    B, S, D = h.shape
    T = B * S
    f32 = jnp.float32
    xf = h.reshape(T, D)
    glog = (xf @ w_group).astype(f32) + b_group.astype(f32)
    gprob = jax.nn.softmax(glog, axis=-1)
    gsel = jnp.argmax(glog, axis=-1)
    p_g = jnp.take_along_axis(gprob, gsel[:, None], axis=-1)
    elog = ((xf @ w_router).astype(f32) + b_router.astype(f32)).reshape(T, N_GROUPS, EXPERTS_PER_GROUP)
    elog_g = jnp.take_along_axis(elog, gsel[:, None, None], axis=1)[:, 0]
    top_v, top_i = lax.top_k(elog_g, TOP_K)
    wts = jax.nn.softmax(top_v, axis=-1) * p_g
    eid = gsel[:, None] * EXPERTS_PER_GROUP + top_i
    A = T * TOP_K
    e_flat = eid.reshape(A)
    w_flat = wts.reshape(A)
    tok = jnp.arange(A, dtype=jnp.int32) // TOP_K
    order = jnp.argsort(e_flat)
    e_sorted = e_flat[order]
    counts = jnp.bincount(e_flat, length=N_EXPERTS)
    padded = (counts + MOE_BLOCK - 1) // MOE_BLOCK * MOE_BLOCK
    pad_end = jnp.cumsum(padded)
    pad_start = pad_end - padded
    start = jnp.cumsum(counts) - counts
    dest = pad_start[e_sorted] + (jnp.arange(A) - start[e_sorted])
    P = (A + N_EXPERTS * (MOE_BLOCK - 1) + MOE_BLOCK - 1) // MOE_BLOCK * MOE_BLOCK
    NB = P // MOE_BLOCK
    slot_tok = jnp.full((P,), T, jnp.int32).at[dest].set(tok[order])
    slot_w = jnp.zeros((P,), f32).at[dest].set(w_flat[order])
    blk_e = jnp.minimum(jnp.searchsorted(pad_end, jnp.arange(NB) * MOE_BLOCK, side='right'), N_EXPERTS - 1)
    x_pad = jnp.concatenate([xf, jnp.zeros((1, D), xf.dtype)], axis=0)
    xb = x_pad[slot_tok].reshape(NB, MOE_BLOCK, D)

    def expert_block(args):
        xblk, e = args
        return (jax.nn.silu(xblk @ w_gate[e]) * (xblk @ w_up[e])) @ w_down[e]

    yb = lax.map(expert_block, (xb, blk_e)).reshape(P, D)
    y = jax.ops.segment_sum(yb * slot_w[:, None].astype(yb.dtype), slot_tok, num_segments=T + 1)[:T]
    return y.reshape(B, S, D)


def setup_inputs(seed: int = 0) -> dict:
    key = jax.random.key(seed)
    ks = jax.random.split(key, 32)
    f32 = jnp.float32
    L, D, H = DEPTH, D_MODEL, DN_HEADS

    def nrm(k, shape, scale):
        return jax.random.normal(k, shape, f32) * scale

    a0 = jax.random.uniform(ks[12], (L, RG_WIDTH), f32, 0.9, 0.999)
    root = a0 ** (1.0 / RG_C)
    dt = jnp.exp(jax.random.uniform(ks[15], (L, H), f32, math.log(1e-3), math.log(1e-1)))
    return {
        'x': nrm(ks[0], (BATCH, SEQ, D), 1.0),
        'c': nrm(ks[1], (BATCH, D), 1.0),
        'w_ada': nrm(ks[2], (L, D, 6 * D), D ** -0.5),
        'b_ada': nrm(ks[3], (L, 6 * D), 0.01),
        'norm1_w': 1.0 + nrm(ks[4], (L, D), 0.01),
        'w_in': nrm(ks[5], (L, D, D_IN), D ** -0.5),
        'rg_conv_w': nrm(ks[6], (L, CONV_WIDTH, RG_WIDTH), CONV_WIDTH ** -0.5),
        'rg_conv_b': nrm(ks[7], (L, RG_WIDTH), 0.01),
        'rg_gate_a_w': nrm(ks[8], (L, RG_BLOCKS, RG_BLOCK_DIM, RG_BLOCK_DIM), RG_BLOCK_DIM ** -0.5),
        'rg_gate_a_b': nrm(ks[9], (L, RG_BLOCKS, RG_BLOCK_DIM), 0.01),
        'rg_gate_x_w': nrm(ks[10], (L, RG_BLOCKS, RG_BLOCK_DIM, RG_BLOCK_DIM), RG_BLOCK_DIM ** -0.5),
        'rg_gate_x_b': nrm(ks[11], (L, RG_BLOCKS, RG_BLOCK_DIM), 0.01),
        'rg_lambda': jnp.log(root) - jnp.log1p(-root),
        'dn_conv_w': nrm(ks[13], (L, CONV_WIDTH, DN_CONV_CH), CONV_WIDTH ** -0.5),
        'dn_a_log': jnp.log(jax.random.uniform(ks[14], (L, H), f32, 1.0, 16.0)),
        'dn_dt_bias': dt + jnp.log(-jnp.expm1(-dt)),
        'dn_norm_w': 1.0 + nrm(ks[16], (L, DN_DV), 0.01),
        'w_branch_rg': nrm(ks[17], (L, RG_WIDTH, D), RG_WIDTH ** -0.5),
        'w_branch_dn': nrm(ks[18], (L, H * DN_DV, D), (H * DN_DV) ** -0.5),
        'w_out': nrm(ks[19], (L, D, D), D ** -0.5),
        'norm2_w': 1.0 + nrm(ks[20], (L, D), 0.01),
        'moe_w_group': nrm(ks[21], (L, D, N_GROUPS), D ** -0.5),
        'moe_b_group': nrm(ks[22], (L, N_GROUPS), 0.01),
        'moe_w_router': nrm(ks[23], (L, D, N_EXPERTS), D ** -0.5),
        'moe_b_router': nrm(ks[24], (L, N_EXPERTS), 0.01),
        'moe_w_gate': nrm(ks[25], (L, N_EXPERTS, D, D_EXPERT), D ** -0.5),
        'moe_w_up': nrm(ks[26], (L, N_EXPERTS, D, D_EXPERT), D ** -0.5),
        'moe_w_down': nrm(ks[27], (L, N_EXPERTS, D_EXPERT, D), D_EXPERT ** -0.5),
        'final_norm_w': 1.0 + nrm(ks[28], (D,), 0.01),
    }


def reference(x, c, w_ada, b_ada, norm1_w, w_in, rg_conv_w, rg_conv_b, rg_gate_a_w, rg_gate_a_b, rg_gate_x_w, rg_gate_x_b, rg_lambda, dn_conv_w, dn_a_log, dn_dt_bias, dn_norm_w, w_branch_rg, w_branch_dn, w_out, norm2_w, moe_w_group, moe_b_group, moe_w_router, moe_b_router, moe_w_gate, moe_w_up, moe_w_down, final_norm_w):
    c_act = jax.nn.silu(c)
    for l in range(DEPTH):
        mod = c_act @ w_ada[l] + b_ada[l]
        shift1, scale1, gate1, shift2, scale2, gate2 = jnp.split(mod[:, None, :], 6, axis=-1)
        h = rms_norm(x, norm1_w[l]) * (1.0 + scale1) + shift1
        x = x + gate1 * hybrid_mixer(h, w_in[l], rg_conv_w[l], rg_conv_b[l], rg_gate_a_w[l], rg_gate_a_b[l], rg_gate_x_w[l], rg_gate_x_b[l], rg_lambda[l], dn_conv_w[l], dn_a_log[l], dn_dt_bias[l], dn_norm_w[l], w_branch_rg[l], w_branch_dn[l], w_out[l])
        h = rms_norm(x, norm2_w[l]) * (1.0 + scale2) + shift2
        x = x + gate2 * hier_moe(h, moe_w_group[l], moe_b_group[l], moe_w_router[l], moe_b_router[l], moe_w_gate[l], moe_w_up[l], moe_w_down[l])
    return rms_norm(x, final_norm_w)
```

```python
import functools

import jax
import jax.numpy as jnp
from jax import lax
from jax.experimental import pallas as pl
from jax.experimental.pallas import tpu as pltpu

F32 = jnp.float32
BF16 = jnp.bfloat16
U32 = jnp.uint32
I32 = jnp.int32
HIGHEST = lax.Precision.HIGHEST

NORM_EPS = 1e-6
RG_C = 8.0
RG_BLOCKS = 4
CONV_WIDTH = 4
DN_HEADS = 8
DN_DK = 128
DN_CHUNK = 64
N_GROUPS = 4
EXPERTS_PER_GROUP = 8
N_EXPERTS = N_GROUPS * EXPERTS_PER_GROUP
MOE_BLOCK = 128
LANES = 128
SUBLANES = 8
HALO = SUBLANES
VMEM_LIMIT = 48 * 1024 * 1024


def _sigmoid(x):
    return 1.0 / (1.0 + jnp.exp(-x))


def _silu(x):
    return x * _sigmoid(x)


def _softplus(x):
    return jnp.maximum(x, 0.0) + jnp.log(1.0 + jnp.exp(-jnp.abs(x)))


def _gelu_tanh(x):
    return x * (0.5 * (1.0 + jnp.tanh(0.7978845608028654 * (x + 0.044715 * (x * x * x)))))


def _mm(a, b):
    return jnp.dot(a.astype(BF16), b.astype(BF16), preferred_element_type=F32)


def _mm_nt(a, b):
    return lax.dot_general(a.astype(BF16), b.astype(BF16), (((1,), (1,)), ((), ())), preferred_element_type=F32)


def _mm_tn(a, b):
    return lax.dot_general(a.astype(BF16), b.astype(BF16), (((0,), (0,)), ((), ())), preferred_element_type=F32)


def _params(*sem):
    return pltpu.CompilerParams(dimension_semantics=sem, vmem_limit_bytes=VMEM_LIMIT)


def _ada_kernel(c_ref, w_ref, b_ref, o_ref):
    o_ref[...] = jnp.dot(_silu(c_ref[...]), w_ref[...], preferred_element_type=F32, precision=HIGHEST) + b_ref[...]


def _ada(c, w_ada, b_ada):
    B, D = c.shape
    N = w_ada.shape[1]
    rows = -(-B // SUBLANES) * SUBLANES
    cp = jnp.pad(c, ((0, rows - B), (0, 0)))
    tn = 1536
    out = pl.pallas_call(
        _ada_kernel,
        grid=(N // tn,),
        in_specs=[
            pl.BlockSpec((rows, D), lambda n: (0, 0)),
            pl.BlockSpec((D, tn), lambda n: (0, n)),
            pl.BlockSpec((1, tn), lambda n: (0, n)),
        ],
        out_specs=pl.BlockSpec((rows, tn), lambda n: (0, n)),
        out_shape=jax.ShapeDtypeStruct((rows, N), F32),
        compiler_params=_params("arbitrary"),
        name="ada",
    )(cp, w_ada, b_ada.reshape(1, N))
    return out[:B]


def _inproj_kernel(x_ref, mod_ref, nw_ref, w_ref, ws_ref, o_ref, os_ref, *, tn):
    x = x_ref[...]
    y = x * lax.rsqrt(jnp.mean(x * x, axis=-1, keepdims=True) + NORM_EPS) * nw_ref[...]
    h = (y * (1.0 + mod_ref[1:2, :]) + mod_ref[0:1, :]).astype(BF16)
    for n in range(w_ref.shape[1] // tn):
        o_ref[:, n * tn:(n + 1) * tn] = jnp.dot(h, w_ref[:, n * tn:(n + 1) * tn], preferred_element_type=F32).astype(BF16)
    os_ref[...] = jnp.dot(h, ws_ref[...], preferred_element_type=F32)


def _inproj(x2, mod3, norm_w, w_main, w_small, S, tm):
    T, D = x2.shape
    N = w_main.shape[1]
    spb = S // tm
    return pl.pallas_call(
        functools.partial(_inproj_kernel, tn=1024),
        grid=(T // tm,),
        in_specs=[
            pl.BlockSpec((tm, D), lambda i: (i, 0)),
            pl.BlockSpec((None, 6, D), lambda i: (i // spb, 0, 0)),
            pl.BlockSpec((1, D), lambda i: (0, 0)),
            pl.BlockSpec((D, N), lambda i: (0, 0), pipeline_mode=pl.Buffered(1)),
            pl.BlockSpec((D, LANES), lambda i: (0, 0)),
        ],
        out_specs=[
            pl.BlockSpec((tm, N), lambda i: (i, 0)),
            pl.BlockSpec((tm, LANES), lambda i: (i, 0)),
        ],
        out_shape=[jax.ShapeDtypeStruct((T, N), BF16), jax.ShapeDtypeStruct((T, LANES), F32)],
        compiler_params=_params("arbitrary"),
        name="inproj",
    )(x2, mod3, norm_w, w_main, w_small)


def _causal_conv(src_ref, buf, w_ref, c0, width, tm):
    buf[HALO:HALO + tm, :] = src_ref[...].astype(F32)
    first = HALO - (CONV_WIDTH - 1)
    acc = w_ref[0:1, c0:c0 + width] * buf[first:first + tm, :]
    for j in range(1, CONV_WIDTH):
        acc = acc + w_ref[j:j + 1, c0:c0 + width] * buf[first + j:first + j + tm, :]
    buf[0:HALO, :] = buf[tm:tm + HALO, :]
    return acc


def _rglru_kernel(x_ref, y_ref, cw_ref, cb_ref, wa_ref, ba_ref, wx_ref, bx_ref, lam_ref, o_ref,
                  xbuf, a_s, b_s, h_s):
    tm, W = x_ref.shape
    bd = W // RG_BLOCKS

    @pl.when(pl.program_id(1) == 0)
    def _():
        xbuf[0:HALO, :] = jnp.zeros((HALO, W), F32)
        h_s[...] = jnp.zeros_like(h_s)

    xc = _causal_conv(x_ref, xbuf, cw_ref, 0, W, tm) + cb_ref[...]
    xb = xc.astype(BF16)
    r = jnp.concatenate([jnp.dot(xb[:, g * bd:(g + 1) * bd], wa_ref[g], preferred_element_type=F32) for g in range(RG_BLOCKS)], axis=1)
    i = jnp.concatenate([jnp.dot(xb[:, g * bd:(g + 1) * bd], wx_ref[g], preferred_element_type=F32) for g in range(RG_BLOCKS)], axis=1)
    r = _sigmoid(r + ba_ref[...])
    i = _sigmoid(i + bx_ref[...])
    log_a = (-RG_C) * r * _softplus(-lam_ref[...])
    a = jnp.exp(log_a)
    b = jnp.sqrt(1.0 - jnp.exp(2.0 * log_a)) * (i * xc)

    row = lax.broadcasted_iota(I32, (tm, W), 0) & (SUBLANES - 1)
    s = 1
    while s < SUBLANES:
        keep = row >= s
        b = jnp.where(keep, a * pltpu.roll(b, s, 0) + b, b)
        a = jnp.where(keep, a * pltpu.roll(a, s, 0), a)
        s *= 2
    a_s[...] = a
    b_s[...] = b

    def group(g, h):
        r0 = pl.multiple_of(g * SUBLANES, SUBLANES)
        hg = a_s[pl.ds(r0, SUBLANES), :] * h + b_s[pl.ds(r0, SUBLANES), :]
        b_s[pl.ds(r0, SUBLANES), :] = hg
        return jnp.broadcast_to(hg[SUBLANES - 1:SUBLANES, :], (SUBLANES, W))

    h_s[...] = lax.fori_loop(0, tm // SUBLANES, group, h_s[...], unroll=4)
    o_ref[...] = (b_s[...] * _gelu_tanh(y_ref[...].astype(F32))).astype(BF16)


def _rglru(proj, cw, cb, wa, ba, wx, bx, lam, B, S, tm):
    T = proj.shape[0]
    W = cw.shape[1]
    spb = S // tm
    bd = W // RG_BLOCKS
    row = lambda: pl.BlockSpec((1, W), lambda b, j: (0, 0))
    return pl.pallas_call(
        _rglru_kernel,
        grid=(B, spb),
        in_specs=[
            pl.BlockSpec((tm, W), lambda b, j: (b * spb + j, 0)),
            pl.BlockSpec((tm, W), lambda b, j: (b * spb + j, 1)),
            pl.BlockSpec((CONV_WIDTH, W), lambda b, j: (0, 0)),
            row(),
            pl.BlockSpec((RG_BLOCKS, bd, bd), lambda b, j: (0, 0, 0)),
            row(),
            pl.BlockSpec((RG_BLOCKS, bd, bd), lambda b, j: (0, 0, 0)),
            row(),
            row(),
        ],
        out_specs=pl.BlockSpec((tm, W), lambda b, j: (b * spb + j, 0)),
        out_shape=jax.ShapeDtypeStruct((T, W), BF16),
        scratch_shapes=[
            pltpu.VMEM((tm + HALO, W), F32),
            pltpu.VMEM((tm, W), F32),
            pltpu.VMEM((tm, W), F32),
            pltpu.VMEM((SUBLANES, W), F32),
        ],
        compiler_params=_params("arbitrary", "arbitrary"),
        name="rglru",
    )(proj, proj, cw, cb, wa, ba, wx, bx, lam)


def _delta_kernel(q_ref, k_ref, v_ref, z_ref, ab_ref, abt_ref, cw_ref, alr_ref, dtr_ref, alc_ref, dtc_ref, nw_ref, o_ref,
                  qbuf, kbuf, vbuf, q_s, k_s, v_s, o_s, gc_s, gr_s, bt_s, st_s):
    tm, HD = q_ref.shape
    C = DN_CHUNK
    DK = DN_DK
    n_chunks = tm // C

    @pl.when(pl.program_id(1) == 0)
    def _():
        for buf in (qbuf, kbuf, vbuf):
            buf[0:HALO, :] = jnp.zeros((HALO, HD), F32)
        st_s[...] = jnp.zeros_like(st_s)

    q_s[...] = _silu(_causal_conv(q_ref, qbuf, cw_ref, 0, HD, tm))
    k_s[...] = _silu(_causal_conv(k_ref, kbuf, cw_ref, HD, HD, tm))
    v_s[...] = _silu(_causal_conv(v_ref, vbuf, cw_ref, 2 * HD, HD, tm))
    for h in range(DN_HEADS):
        sl = slice(h * DK, (h + 1) * DK)
        qh = q_s[:, sl]
        q_s[:, sl] = qh * lax.rsqrt(jnp.sum(qh * qh, axis=-1, keepdims=True) + NORM_EPS) * (DK ** -0.5)
        kh = k_s[:, sl]
        k_s[:, sl] = kh * lax.rsqrt(jnp.sum(kh * kh, axis=-1, keepdims=True) + NORM_EPS)

    ab = ab_ref[...]
    bt_s[...] = _sigmoid(ab)
    g_rows = -jnp.exp(alr_ref[...]) * _softplus(ab + dtr_ref[...])
    g_cols = -jnp.exp(alc_ref[...]) * _softplus(abt_ref[...] + dtc_ref[...])
    ri = lax.broadcasted_iota(I32, (C, C), 0)
    ci = lax.broadcasted_iota(I32, (C, C), 1)
    causal = ri >= ci
    strict = ri > ci
    lower = causal.astype(F32)
    upper = (ri <= ci).astype(F32)
    for c in range(n_chunks):
        gc_s[c * C:(c + 1) * C, :] = jnp.dot(lower, g_rows[c * C:(c + 1) * C, :], preferred_element_type=F32, precision=HIGHEST)
        gr_s[c] = jnp.dot(g_cols[:, c * C:(c + 1) * C], upper, preferred_element_type=F32, precision=HIGHEST)
    eye = (ri == ci).astype(F32)
    diag_blk = strict & ((ri >> 3) == (ci >> 3))
    merge_masks = []
    sh = 3
    while (1 << sh) < C:
        merge_masks.append(((ri >> (sh + 1)) == (ci >> (sh + 1))) & (((ri >> sh) & 1) == 1) & (((ci >> sh) & 1) == 0))
        sh += 1

    def chunk(c, carry):
        r0 = pl.multiple_of(c * C, C)
        g_all = gc_s[pl.ds(r0, C), :]
        g_last = gc_s[pl.ds(r0 + C - 1, 1), :]
        eg_all = jnp.exp(g_all)
        ekd_all = jnp.exp(g_last - g_all)
        ge_all = jnp.exp(g_last)
        beta_all = bt_s[pl.ds(r0, C), :]
        gr_all = gr_s[c]
        for h in range(DN_HEADS):
            sl = slice(h * DK, (h + 1) * DK)
            gl = DN_HEADS + h
            qh = q_s[pl.ds(r0, C), sl]
            kh = k_s[pl.ds(r0, C), sl]
            vh = v_s[pl.ds(r0, C), sl]
            beta = beta_all[:, h:h + 1]
            eg = eg_all[:, gl:gl + 1]
            decay = jnp.exp(jnp.where(causal, g_all[:, gl:gl + 1] - gr_all[gl:gl + 1, :], -jnp.inf))
            kb = kh * beta
            x = jnp.where(strict, _mm_nt(kb, kh) * decay, 0.0)
            p = jnp.where(diag_blk, -x, 0.0)
            t = eye + p
            p = _mm(p, p)
            t = t + _mm(t, p)
            p = _mm(p, p)
            t = t + _mm(t, p)
            for m in merge_masks:
                t = t - _mm(t, _mm(jnp.where(m, x, 0.0), t))
            sol = _mm(t, jnp.concatenate([vh * beta, kb * eg], axis=1))
            u = sol[:, :DK]
            w = sol[:, DK:]
            qk = jnp.where(causal, _mm_nt(qh, kh) * decay, 0.0)
            st = st_s[h]
            v_new = u - _mm(w, st)
            o_s[pl.ds(r0, C), sl] = _mm(qh * eg, st) + _mm(qk, v_new)
            st_s[h] = st * ge_all[:, gl:gl + 1] + _mm_tn(kh * ekd_all[:, gl:gl + 1], v_new)
        return carry

    lax.fori_loop(0, n_chunks, chunk, 0)

    for h in range(DN_HEADS):
        sl = slice(h * DK, (h + 1) * DK)
        oh = o_s[:, sl]
        y = oh * lax.rsqrt(jnp.mean(oh * oh, axis=-1, keepdims=True) + NORM_EPS) * nw_ref[...]
        o_ref[:, sl] = (y * _silu(z_ref[:, sl].astype(F32))).astype(BF16)


def _delta(proj, ab, abt, cw, alr, dtr, alc, dtc, nw, B, S, tm):
    T = proj.shape[0]
    HD = DN_HEADS * DN_DK
    spb = S // tm
    n_chunks = tm // DN_CHUNK
    col = lambda k: pl.BlockSpec((tm, HD), lambda b, j: (b * spb + j, k))
    const = lambda shape: pl.BlockSpec(shape, lambda b, j: (0,) * len(shape))
    return pl.pallas_call(
        _delta_kernel,
        grid=(B, spb),
        in_specs=[
            col(2), col(3), col(4), col(5),
            pl.BlockSpec((tm, LANES), lambda b, j: (b * spb + j, 0)),
            pl.BlockSpec((None, 2 * DN_HEADS, tm), lambda b, j: (b, 0, j)),
            const((CONV_WIDTH, 3 * HD)),
            const((1, LANES)), const((1, LANES)),
            const((2 * DN_HEADS, 1)), const((2 * DN_HEADS, 1)),
            const((1, DN_DK)),
        ],
        out_specs=pl.BlockSpec((tm, HD), lambda b, j: (b * spb + j, 0)),
        out_shape=jax.ShapeDtypeStruct((T, HD), BF16),
        scratch_shapes=[
            pltpu.VMEM((tm + HALO, HD), F32), pltpu.VMEM((tm + HALO, HD), F32), pltpu.VMEM((tm + HALO, HD), F32),
            pltpu.VMEM((tm, HD), F32), pltpu.VMEM((tm, HD), F32), pltpu.VMEM((tm, HD), F32), pltpu.VMEM((tm, HD), F32),
            pltpu.VMEM((tm, LANES), F32),
            pltpu.VMEM((n_chunks, 2 * DN_HEADS, DN_CHUNK), F32),
            pltpu.VMEM((tm, LANES), F32),
            pltpu.VMEM((DN_HEADS, DN_DK, DN_DK), F32),
        ],
        compiler_params=_params("arbitrary", "arbitrary"),
        name="delta",
    )(proj, proj, proj, proj, ab, abt, cw, alr, dtr, alc, dtc, nw)


def _pack_bf16_pairs(h):
    n = h.shape[1] // 2
    hb = h.astype(BF16).astype(F32)
    lo = lax.bitcast_convert_type(hb[:, :n], U32)
    hi = lax.bitcast_convert_type(hb[:, n:], U32)
    return hi | (lo >> 16)


def _unpack_bf16_pairs(p):
    lo = lax.bitcast_convert_type(p << 16, F32)
    hi = lax.bitcast_convert_type(p & jnp.uint32(0xFFFF0000), F32)
    return jnp.concatenate([lo, hi], axis=1).astype(BF16)


def _merge_kernel(yrg_ref, ydn_ref, grg_ref, gdn_ref, x_ref, mod_ref, wbr_ref, wbd_ref, wo_ref, nw_ref, wr_ref,
                  x1_ref, hp_ref, lg_ref):
    m = (_sigmoid(grg_ref[...].astype(F32)) * jnp.dot(yrg_ref[...], wbr_ref[...], preferred_element_type=F32)
         + _sigmoid(gdn_ref[...].astype(F32)) * jnp.dot(ydn_ref[...], wbd_ref[...], preferred_element_type=F32))
    x1 = x_ref[...] + mod_ref[2:3, :] * jnp.dot(m.astype(BF16), wo_ref[...], preferred_element_type=F32)
    x1_ref[...] = x1
    y = x1 * lax.rsqrt(jnp.mean(x1 * x1, axis=-1, keepdims=True) + NORM_EPS) * nw_ref[...]
    h2 = y * (1.0 + mod_ref[4:5, :]) + mod_ref[3:4, :]
    lg_ref[...] = jnp.dot(h2, wr_ref[...], preferred_element_type=F32, precision=HIGHEST)
    hp_ref[...] = _pack_bf16_pairs(h2)


def _merge(y_rg, y_dn, proj, x2, mod3, wbr, wbd, wo, nw, wr, S, tm):
    T, D = x2.shape
    spb = S // tm
    tile = lambda k=0: pl.BlockSpec((tm, D), lambda i: (i, k))
    const = lambda shape: pl.BlockSpec(shape, lambda i: (0,) * len(shape))
    return pl.pallas_call(
        _merge_kernel,
        grid=(T // tm,),
        in_specs=[
            tile(), tile(), tile(6), tile(7), tile(),
            pl.BlockSpec((None, 6, D), lambda i: (i // spb, 0, 0)),
            const((D, D)), const((D, D)), const((D, D)), const((1, D)), const((D, LANES)),
        ],
        out_specs=[tile(), pl.BlockSpec((tm, D // 2), lambda i: (i, 0)), pl.BlockSpec((tm, LANES), lambda i: (i, 0))],
        out_shape=[jax.ShapeDtypeStruct((T, D), F32), jax.ShapeDtypeStruct((T, D // 2), U32),
                   jax.ShapeDtypeStruct((T, LANES), F32)],
        compiler_params=_params("arbitrary"),
        name="merge",
    )(y_rg, y_dn, proj, proj, x2, mod3, wbr, wbd, wo, nw, wr)


def _route_kernel(lg_ref, bias_ref, slab_ref, slabt_ref, cnt_ref, carry):
    tm = lg_ref.shape[0]

    @pl.when(pl.program_id(0) == 0)
    def _():
        carry[...] = jnp.zeros_like(carry)

    lg = lg_ref[...] + bias_ref[...]
    lane = lax.broadcasted_iota(I32, (tm, LANES), 1)
    big = jnp.int32(LANES)
    ninf = -jnp.inf

    def first_max(vals):
        vmax = jnp.max(vals, axis=-1, keepdims=True)
        return vmax, jnp.min(jnp.where(vals == vmax, lane, big), axis=-1, keepdims=True)

    is_g = lane < N_GROUPS
    gmax, gsel = first_max(jnp.where(is_g, lg, ninf))
    p_g = 1.0 / jnp.sum(jnp.where(is_g, jnp.exp(lg - gmax), 0.0), axis=-1, keepdims=True)
    e_lane = lane - N_GROUPS
    in_group = (e_lane >= 0) & (e_lane < N_EXPERTS) & ((e_lane >> 3) == gsel)
    el = jnp.where(in_group, lg, ninf)
    v1, l1 = first_max(el)
    v2, l2 = first_max(jnp.where(lane == l1, ninf, el))
    ex = jnp.exp(v2 - v1)
    w1 = p_g / (1.0 + ex)
    w2 = p_g * ex / (1.0 + ex)
    e1 = l1 - N_GROUPS
    e2 = l2 - N_GROUPS

    hit1 = lane == e1
    hit2 = lane == e2
    onehot = jnp.where(hit1 | hit2, 1.0, 0.0)
    ri = lax.broadcasted_iota(I32, (tm, tm), 0)
    ci = lax.broadcasted_iota(I32, (tm, tm), 1)
    before = jnp.where(ri > ci, 1.0, 0.0).astype(BF16)
    cum = jnp.dot(before, onehot.astype(BF16), preferred_element_type=F32) + carry[...]
    r1 = jnp.sum(jnp.where(hit1, cum, 0.0), axis=-1, keepdims=True)
    r2 = jnp.sum(jnp.where(hit2, cum, 0.0), axis=-1, keepdims=True)
    carry[...] = carry[...] + jnp.sum(onehot, axis=0, keepdims=True)
    cnt_ref[...] = carry[...]

    slab = jnp.where(lane == 0, e1.astype(F32), 0.0)
    slab = jnp.where(lane == 1, e2.astype(F32), slab)
    slab = jnp.where(lane == 2, r1, slab)
    slab = jnp.where(lane == 3, r2, slab)
    slab = jnp.where(lane == 4, w1, slab)
    slab = jnp.where(lane == 5, w2, slab)
    slab_ref[...] = slab
    slabt_ref[...] = jnp.transpose(slab)[0:SUBLANES, :]


def _route(logits, bias, tm):
    T = logits.shape[0]
    return pl.pallas_call(
        _route_kernel,
        grid=(T // tm,),
        in_specs=[pl.BlockSpec((tm, LANES), lambda i: (i, 0)), pl.BlockSpec((1, LANES), lambda i: (0, 0))],
        out_specs=[pl.BlockSpec((tm, LANES), lambda i: (i, 0)), pl.BlockSpec((SUBLANES, tm), lambda i: (0, i)),
                   pl.BlockSpec((1, LANES), lambda i: (0, 0))],
        out_shape=[jax.ShapeDtypeStruct((T, LANES), F32), jax.ShapeDtypeStruct((SUBLANES, T), F32),
                   jax.ShapeDtypeStruct((1, LANES), F32)],
        scratch_shapes=[pltpu.VMEM((1, LANES), F32)],
        compiler_params=_params("arbitrary"),
        name="route",
    )(logits, bias)


def _row_copy(src, dst, s, d, sem):
    return pltpu.make_async_copy(src.at[pl.ds(s, 1), :], dst.at[pl.ds(d, 1), :], sem)


def _sort_kernel(dest_ref, hp_ref, xs_in_ref, xs_ref, sem, *, tb):
    del xs_in_ref
    i = pl.program_id(0)
    base = i * tb

    def issue(r, carry):
        _row_copy(hp_ref, xs_ref, base + r, dest_ref[0, r], sem).start()
        _row_copy(hp_ref, xs_ref, base + r, dest_ref[0, tb + r], sem).start()
        return carry

    lax.fori_loop(0, tb, issue, 0, unroll=8)

    def drain(r, carry):
        _row_copy(hp_ref, xs_ref, 0, 0, sem).wait()
        _row_copy(hp_ref, xs_ref, 0, 0, sem).wait()
        return carry

    lax.fori_loop(0, tb, drain, 0, unroll=8)


def _sort(dest3, hp, xs0, tb):
    T, Dh = hp.shape
    return pl.pallas_call(
        functools.partial(_sort_kernel, tb=tb),
        grid=(T // tb,),
        in_specs=[
            pl.BlockSpec((None, 1, 2 * tb), lambda i: (i, 0, 0), memory_space=pltpu.SMEM),
            pl.BlockSpec(memory_space=pl.ANY),
            pl.BlockSpec(memory_space=pl.ANY),
        ],
        out_specs=pl.BlockSpec(memory_space=pl.ANY),
        out_shape=jax.ShapeDtypeStruct(xs0.shape, xs0.dtype),
        scratch_shapes=[pltpu.SemaphoreType.DMA],
        input_output_aliases={2: 0},
        compiler_params=pltpu.CompilerParams(dimension_semantics=("arbitrary",), has_side_effects=True),
        name="sort",
    )(dest3, hp, xs0)


def _experts_kernel(be_ref, nv_ref, x_ref, wg_ref, wu_ref, wd_ref, y_ref, wg_b, wu_b, wd_b):
    i = pl.program_id(0)
    valid = i < nv_ref[0]

    @pl.when(valid & ((i == 0) | (be_ref[i] != be_ref[jnp.maximum(i - 1, 0)])))
    def _():
        wg_b[...] = wg_ref[...].astype(BF16)
        wu_b[...] = wu_ref[...].astype(BF16)
        wd_b[...] = wd_ref[...].astype(BF16)

    @pl.when(valid)
    def _():
        x = _unpack_bf16_pairs(x_ref[...])
        g = jnp.dot(x, wg_b[...], preferred_element_type=F32)
        u = jnp.dot(x, wu_b[...], preferred_element_type=F32)
        y_ref[...] = jnp.dot((_silu(g) * u).astype(BF16), wd_b[...], preferred_element_type=F32)

    @pl.when(jnp.logical_not(valid))
    def _():
        y_ref[...] = jnp.zeros_like(y_ref)


def _experts(blk_e, nvb, xs, wg, wu, wd):
    P, Dh = xs.shape
    E, D, DE = wg.shape
    NB = P // MOE_BLOCK
    last = lambda i, nv: jnp.minimum(i, nv[0] - 1)
    grid_spec = pltpu.PrefetchScalarGridSpec(
        num_scalar_prefetch=2,
        grid=(NB,),
        in_specs=[
            pl.BlockSpec((MOE_BLOCK, Dh), lambda i, be, nv: (last(i, nv), 0)),
            pl.BlockSpec((None, D, DE), lambda i, be, nv: (be[last(i, nv)], 0, 0)),
            pl.BlockSpec((None, D, DE), lambda i, be, nv: (be[last(i, nv)], 0, 0)),
            pl.BlockSpec((None, DE, D), lambda i, be, nv: (be[last(i, nv)], 0, 0)),
        ],
        out_specs=pl.BlockSpec((MOE_BLOCK, D), lambda i, be, nv: (i, 0)),
        scratch_shapes=[pltpu.VMEM((D, DE), BF16), pltpu.VMEM((D, DE), BF16), pltpu.VMEM((DE, D), BF16)],
    )
    return pl.pallas_call(
        _experts_kernel,
        grid_spec=grid_spec,
        out_shape=jax.ShapeDtypeStruct((P, D), F32),
        compiler_params=_params("arbitrary"),
        name="experts",
    )(blk_e, nvb, xs, wg, wu, wd)


def _final_kernel(dest_ref, x1_ref, slab_ref, mod_ref, fw_ref, ys_ref, o_ref, ybuf, sem, *, tb):
    def issue(r, carry):
        _row_copy(ys_ref, ybuf.at[0], dest_ref[0, r], r, sem).start()
        _row_copy(ys_ref, ybuf.at[1], dest_ref[0, tb + r], r, sem).start()
        return carry

    lax.fori_loop(0, tb, issue, 0, unroll=8)

    def drain(r, carry):
        _row_copy(ys_ref, ybuf.at[0], 0, 0, sem).wait()
        _row_copy(ys_ref, ybuf.at[1], 0, 0, sem).wait()
        return carry

    lax.fori_loop(0, tb, drain, 0, unroll=8)

    slab = slab_ref[...]
    y = slab[:, 4:5] * ybuf[0] + slab[:, 5:6] * ybuf[1]
    x2 = x1_ref[...] + mod_ref[5:6, :] * y
    o_ref[...] = x2 * lax.rsqrt(jnp.mean(x2 * x2, axis=-1, keepdims=True) + NORM_EPS) * fw_ref[...]


def _final(dest3, x1, slab, mod3, fw, ys, S, tb):
    T, D = x1.shape
    spb = S // tb
    return pl.pallas_call(
        functools.partial(_final_kernel, tb=tb),
        grid=(T // tb,),
        in_specs=[
            pl.BlockSpec((None, 1, 2 * tb), lambda i: (i, 0, 0), memory_space=pltpu.SMEM),
            pl.BlockSpec((tb, D), lambda i: (i, 0)),
            pl.BlockSpec((tb, LANES), lambda i: (i, 0)),
            pl.BlockSpec((None, 6, D), lambda i: (i // spb, 0, 0)),
            pl.BlockSpec((1, D), lambda i: (0, 0)),
            pl.BlockSpec(memory_space=pl.ANY),
        ],
        out_specs=pl.BlockSpec((tb, D), lambda i: (i, 0)),
        out_shape=jax.ShapeDtypeStruct((T, D), F32),
        scratch_shapes=[pltpu.VMEM((2, tb, D), F32), pltpu.SemaphoreType.DMA],
        compiler_params=_params("arbitrary"),
        name="final",
    )(dest3, x1, slab, mod3, fw, ys)


def _lane_row(vals, offset):
    return jnp.zeros((1, LANES), F32).at[0, offset:offset + vals.shape[0]].set(vals.astype(F32))


def _forward(x, c, w_ada, b_ada, norm1_w, w_in, rg_conv_w, rg_conv_b, rg_gate_a_w, rg_gate_a_b, rg_gate_x_w, rg_gate_x_b,
             rg_lambda, dn_conv_w, dn_a_log, dn_dt_bias, dn_norm_w, w_branch_rg, w_branch_dn, w_out, norm2_w,
             moe_w_group, moe_b_group, moe_w_router, moe_b_router, moe_w_gate, moe_w_up, moe_w_down, final_norm_w,
             tm_proj=512, tm_seq=256, tm_tok=512):
    B, S, D = x.shape
    T = B * S
    H = DN_HEADS
    x2 = x.reshape(T, D)
    mod3 = _ada(c, w_ada[0], b_ada[0]).reshape(B, 6, D)

    w = w_in[0]
    n_wide = 6 * D
    w_main = jnp.concatenate([w[:, :n_wide], w[:, n_wide + 2 * H:]], axis=1).astype(BF16)
    w_small = jnp.pad(w[:, n_wide:n_wide + 2 * H], ((0, 0), (0, LANES - 2 * H))).astype(BF16)
    proj, ab = _inproj(x2, mod3, norm1_w, w_main, w_small, S, tm_proj)

    y_rg = _rglru(proj, rg_conv_w[0], rg_conv_b, rg_gate_a_w[0].astype(BF16), rg_gate_a_b.reshape(1, -1),
                  rg_gate_x_w[0].astype(BF16), rg_gate_x_b.reshape(1, -1), rg_lambda, B, S, tm_seq)

    abt = ab[:, :2 * H].reshape(B, S, 2 * H).transpose(0, 2, 1)
    zeros_h = jnp.zeros((H,), F32)
    col = lambda v: jnp.concatenate([zeros_h, v.astype(F32)]).reshape(2 * H, 1)
    y_dn = _delta(proj, ab, abt, dn_conv_w[0], _lane_row(dn_a_log[0], H), _lane_row(dn_dt_bias[0], H),
                  col(dn_a_log[0]), col(dn_dt_bias[0]), dn_norm_w, B, S, tm_seq)

    w_route = jnp.pad(jnp.concatenate([moe_w_group[0], moe_w_router[0]], axis=1), ((0, 0), (0, LANES - N_GROUPS - N_EXPERTS)))
    x1, hp, logits = _merge(y_rg, y_dn, proj, x2, mod3, w_branch_rg[0].astype(BF16), w_branch_dn[0].astype(BF16),
                            w_out[0].astype(BF16), norm2_w, w_route, S, tm_proj)

    bias = _lane_row(jnp.concatenate([moe_b_group[0], moe_b_router[0]]), 0)
    slab, slabt, cnt = _route(logits, bias, tm_tok)

    eid = slabt[0:2].astype(I32)
    rank = slabt[2:4].astype(I32)
    counts = cnt[0, :N_EXPERTS].astype(I32)
    padded = (counts + MOE_BLOCK - 1) // MOE_BLOCK * MOE_BLOCK
    pad_end = jnp.cumsum(padded)
    pad_start = pad_end - padded
    seg = jnp.sum(jnp.where(eid[:, :, None] == jnp.arange(N_EXPERTS, dtype=I32), pad_start, 0), axis=-1)
    dest = seg + rank
    nt = T // tm_tok
    dest3 = dest.reshape(2, nt, tm_tok).transpose(1, 0, 2).reshape(nt, 1, 2 * tm_tok)
    A = 2 * T
    P = (A + N_EXPERTS * (MOE_BLOCK - 1) + MOE_BLOCK - 1) // MOE_BLOCK * MOE_BLOCK
    NB = P // MOE_BLOCK
    starts = jnp.arange(NB, dtype=I32) * MOE_BLOCK
    blk_e = jnp.minimum(jnp.sum(pad_end[None, :] <= starts[:, None], axis=1), N_EXPERTS - 1).astype(I32)
    nvb = (pad_end[-1:] // MOE_BLOCK).astype(I32)

    xs = _sort(dest3, hp, jnp.zeros((P, D // 2), U32), tm_tok)
    ys = _experts(blk_e, nvb, xs, moe_w_gate[0], moe_w_up[0], moe_w_down[0])
    out = _final(dest3, x1, slab, mod3, final_norm_w.reshape(1, D), ys, S, tm_tok)
    return out.reshape(B, S, D)


def kernel(x, c, w_ada, b_ada, norm1_w, w_in, rg_conv_w, rg_conv_b, rg_gate_a_w, rg_gate_a_b, rg_gate_x_w, rg_gate_x_b, rg_lambda, dn_conv_w, dn_a_log, dn_dt_bias, dn_norm_w, w_branch_rg, w_branch_dn, w_out, norm2_w, moe_w_group, moe_b_group, moe_w_router, moe_b_router, moe_w_gate, moe_w_up, moe_w_down, final_norm_w):
    return _forward(x, c, w_ada, b_ada, norm1_w, w_in, rg_conv_w, rg_conv_b, rg_gate_a_w, rg_gate_a_b, rg_gate_x_w, rg_gate_x_b, rg_lambda, dn_conv_w, dn_a_log, dn_dt_bias, dn_norm_w, w_branch_rg, w_branch_dn, w_out, norm2_w, moe_w_group, moe_b_group, moe_w_router, moe_b_router, moe_w_gate, moe_w_up, moe_w_down, final_norm_w)
```

```python
import functools

import jax
import jax.numpy as jnp
from jax import lax
from jax.experimental import pallas as pl
from jax.experimental.pallas import tpu as pltpu

F32 = jnp.float32
BF16 = jnp.bfloat16
U32 = jnp.uint32
I32 = jnp.int32
HIGHEST = lax.Precision.HIGHEST

NORM_EPS = 1e-6
RG_C = 8.0
RG_BLOCKS = 4
CONV_WIDTH = 4
DN_HEADS = 8
DN_DK = 128
DN_CHUNK = 64
N_GROUPS = 4
EXPERTS_PER_GROUP = 8
N_EXPERTS = N_GROUPS * EXPERTS_PER_GROUP
MOE_BLOCK = 512
LANES = 128
SUBLANES = 8
HALO = SUBLANES
VMEM_LIMIT = 48 * 1024 * 1024


def _sigmoid(x):
    return 1.0 / (1.0 + jnp.exp(-x))


def _silu(x):
    return x * _sigmoid(x)


def _softplus(x):
    return jnp.maximum(x, 0.0) + jnp.log(1.0 + jnp.exp(-jnp.abs(x)))


def _gelu_tanh(x):
    return x * (0.5 * (1.0 + jnp.tanh(0.7978845608028654 * (x + 0.044715 * (x * x * x)))))


def _mm(a, b):
    return jnp.dot(a.astype(BF16), b.astype(BF16), preferred_element_type=F32)


def _mm_nt(a, b):
    return lax.dot_general(a.astype(BF16), b.astype(BF16), (((1,), (1,)), ((), ())), preferred_element_type=F32)


def _mm_tn(a, b):
    return lax.dot_general(a.astype(BF16), b.astype(BF16), (((0,), (0,)), ((), ())), preferred_element_type=F32)


def _params(*sem):
    return pltpu.CompilerParams(dimension_semantics=sem, vmem_limit_bytes=VMEM_LIMIT)


def _ada_kernel(c_ref, w_ref, b_ref, o_ref):
    o_ref[...] = jnp.dot(_silu(c_ref[...]), w_ref[...], preferred_element_type=F32, precision=HIGHEST) + b_ref[...]


def _ada(c, w_ada, b_ada):
    B, D = c.shape
    N = w_ada.shape[1]
    rows = -(-B // SUBLANES) * SUBLANES
    cp = jnp.pad(c, ((0, rows - B), (0, 0)))
    tn = 1536
    out = pl.pallas_call(
        _ada_kernel,
        grid=(N // tn,),
        in_specs=[
            pl.BlockSpec((rows, D), lambda n: (0, 0)),
            pl.BlockSpec((D, tn), lambda n: (0, n)),
            pl.BlockSpec((1, tn), lambda n: (0, n)),
        ],
        out_specs=pl.BlockSpec((rows, tn), lambda n: (0, n)),
        out_shape=jax.ShapeDtypeStruct((rows, N), F32),
        compiler_params=_params("arbitrary"),
        name="ada",
    )(cp, w_ada, b_ada.reshape(1, N))
    return out[:B]


def _inproj_kernel(x_ref, mod_ref, nw_ref, w_ref, ws_ref, o_ref, os_ref, *, tn):
    x = x_ref[...]
    y = x * lax.rsqrt(jnp.mean(x * x, axis=-1, keepdims=True) + NORM_EPS) * nw_ref[...]
    h = (y * (1.0 + mod_ref[1:2, :]) + mod_ref[0:1, :]).astype(BF16)
    for n in range(w_ref.shape[1] // tn):
        o_ref[:, n * tn:(n + 1) * tn] = jnp.dot(h, w_ref[:, n * tn:(n + 1) * tn], preferred_element_type=F32).astype(BF16)
    os_ref[...] = jnp.dot(h, ws_ref[...], preferred_element_type=F32)


def _inproj(x2, mod3, norm_w, w_main, w_small, S, tm):
    T, D = x2.shape
    N = w_main.shape[1]
    spb = S // tm
    return pl.pallas_call(
        functools.partial(_inproj_kernel, tn=1024),
        grid=(T // tm,),
        in_specs=[
            pl.BlockSpec((tm, D), lambda i: (i, 0)),
            pl.BlockSpec((None, 6, D), lambda i: (i // spb, 0, 0)),
            pl.BlockSpec((1, D), lambda i: (0, 0)),
            pl.BlockSpec((D, N), lambda i: (0, 0), pipeline_mode=pl.Buffered(1)),
            pl.BlockSpec((D, LANES), lambda i: (0, 0)),
        ],
        out_specs=[
            pl.BlockSpec((tm, N), lambda i: (i, 0)),
            pl.BlockSpec((tm, LANES), lambda i: (i, 0)),
        ],
        out_shape=[jax.ShapeDtypeStruct((T, N), BF16), jax.ShapeDtypeStruct((T, LANES), F32)],
        compiler_params=_params("arbitrary"),
        name="inproj",
    )(x2, mod3, norm_w, w_main, w_small)


def _causal_conv(src_ref, buf, w_ref, c0, width, tm):
    buf[HALO:HALO + tm, :] = src_ref[...].astype(F32)
    first = HALO - (CONV_WIDTH - 1)
    acc = w_ref[0:1, c0:c0 + width] * buf[first:first + tm, :]
    for j in range(1, CONV_WIDTH):
        acc = acc + w_ref[j:j + 1, c0:c0 + width] * buf[first + j:first + j + tm, :]
    buf[0:HALO, :] = buf[tm:tm + HALO, :]
    return acc


def _rglru_kernel(x_ref, y_ref, cw_ref, cb_ref, wa_ref, ba_ref, wx_ref, bx_ref, lam_ref, o_ref,
                  xbuf, a_s, b_s, h_s):
    tm, W = x_ref.shape
    bd = W // RG_BLOCKS

    @pl.when(pl.program_id(1) == 0)
    def _():
        xbuf[0:HALO, :] = jnp.zeros((HALO, W), F32)
        h_s[...] = jnp.zeros_like(h_s)

    xc = _causal_conv(x_ref, xbuf, cw_ref, 0, W, tm) + cb_ref[...]
    xb = xc.astype(BF16)
    r = jnp.concatenate([jnp.dot(xb[:, g * bd:(g + 1) * bd], wa_ref[g], preferred_element_type=F32) for g in range(RG_BLOCKS)], axis=1)
    i = jnp.concatenate([jnp.dot(xb[:, g * bd:(g + 1) * bd], wx_ref[g], preferred_element_type=F32) for g in range(RG_BLOCKS)], axis=1)
    r = _sigmoid(r + ba_ref[...])
    i = _sigmoid(i + bx_ref[...])
    log_a = (-RG_C) * r * _softplus(-lam_ref[...])
    a = jnp.exp(log_a)
    b = jnp.sqrt(1.0 - jnp.exp(2.0 * log_a)) * (i * xc)

    row = lax.broadcasted_iota(I32, (tm, W), 0) & (SUBLANES - 1)
    s = 1
    while s < SUBLANES:
        keep = row >= s
        b = jnp.where(keep, a * pltpu.roll(b, s, 0) + b, b)
        a = jnp.where(keep, a * pltpu.roll(a, s, 0), a)
        s *= 2
    a_s[...] = a
    b_s[...] = b

    def group(g, h):
        r0 = pl.multiple_of(g * SUBLANES, SUBLANES)
        hg = a_s[pl.ds(r0, SUBLANES), :] * h + b_s[pl.ds(r0, SUBLANES), :]
        b_s[pl.ds(r0, SUBLANES), :] = hg
        return jnp.broadcast_to(hg[SUBLANES - 1:SUBLANES, :], (SUBLANES, W))

    h_s[...] = lax.fori_loop(0, tm // SUBLANES, group, h_s[...], unroll=4)
    o_ref[...] = (b_s[...] * _gelu_tanh(y_ref[...].astype(F32))).astype(BF16)


def _rglru(proj, cw, cb, wa, ba, wx, bx, lam, B, S, tm):
    T = proj.shape[0]
    W = cw.shape[1]
    spb = S // tm
    bd = W // RG_BLOCKS
    row = lambda: pl.BlockSpec((1, W), lambda b, j: (0, 0))
    return pl.pallas_call(
        _rglru_kernel,
        grid=(B, spb),
        in_specs=[
            pl.BlockSpec((tm, W), lambda b, j: (b * spb + j, 0)),
            pl.BlockSpec((tm, W), lambda b, j: (b * spb + j, 1)),
            pl.BlockSpec((CONV_WIDTH, W), lambda b, j: (0, 0)),
            row(),
            pl.BlockSpec((RG_BLOCKS, bd, bd), lambda b, j: (0, 0, 0)),
            row(),
            pl.BlockSpec((RG_BLOCKS, bd, bd), lambda b, j: (0, 0, 0)),
            row(),
            row(),
        ],
        out_specs=pl.BlockSpec((tm, W), lambda b, j: (b * spb + j, 0)),
        out_shape=jax.ShapeDtypeStruct((T, W), BF16),
        scratch_shapes=[
            pltpu.VMEM((tm + HALO, W), F32),
            pltpu.VMEM((tm, W), F32),
            pltpu.VMEM((tm, W), F32),
            pltpu.VMEM((SUBLANES, W), F32),
        ],
        compiler_params=_params("arbitrary", "arbitrary"),
        name="rglru",
    )(proj, proj, cw, cb, wa, ba, wx, bx, lam)


def _delta_kernel(q_ref, k_ref, v_ref, z_ref, ab_ref, abt_ref, cw_ref, alr_ref, dtr_ref, alc_ref, dtc_ref, nw_ref, o_ref,
                  qbuf, kbuf, vbuf, q_s, k_s, v_s, o_s, gc_s, gr_s, bt_s, st_s):
    tm, HD = q_ref.shape
    C = DN_CHUNK
    DK = DN_DK
    n_chunks = tm // C

    @pl.when(pl.program_id(1) == 0)
    def _():
        for buf in (qbuf, kbuf, vbuf):
            buf[0:HALO, :] = jnp.zeros((HALO, HD), F32)
        st_s[...] = jnp.zeros_like(st_s)

    q_s[...] = _silu(_causal_conv(q_ref, qbuf, cw_ref, 0, HD, tm))
    k_s[...] = _silu(_causal_conv(k_ref, kbuf, cw_ref, HD, HD, tm))
    v_s[...] = _silu(_causal_conv(v_ref, vbuf, cw_ref, 2 * HD, HD, tm))
    for h in range(DN_HEADS):
        sl = slice(h * DK, (h + 1) * DK)
        qh = q_s[:, sl]
        q_s[:, sl] = qh * lax.rsqrt(jnp.sum(qh * qh, axis=-1, keepdims=True) + NORM_EPS) * (DK ** -0.5)
        kh = k_s[:, sl]
        k_s[:, sl] = kh * lax.rsqrt(jnp.sum(kh * kh, axis=-1, keepdims=True) + NORM_EPS)

    ab = ab_ref[...]
    bt_s[...] = _sigmoid(ab)
    g_rows = -jnp.exp(alr_ref[...]) * _softplus(ab + dtr_ref[...])
    g_cols = -jnp.exp(alc_ref[...]) * _softplus(abt_ref[...] + dtc_ref[...])
    ri = lax.broadcasted_iota(I32, (C, C), 0)
    ci = lax.broadcasted_iota(I32, (C, C), 1)
    causal = ri >= ci
    strict = ri > ci
    lower = causal.astype(F32)
    upper = (ri <= ci).astype(F32)
    for c in range(n_chunks):
        gc_s[c * C:(c + 1) * C, :] = jnp.dot(lower, g_rows[c * C:(c + 1) * C, :], preferred_element_type=F32, precision=HIGHEST)
        gr_s[c] = jnp.dot(g_cols[:, c * C:(c + 1) * C], upper, preferred_element_type=F32, precision=HIGHEST)
    eye = (ri == ci).astype(F32)
    diag_blk = strict & ((ri >> 3) == (ci >> 3))
    merge_masks = []
    sh = 3
    while (1 << sh) < C:
        merge_masks.append(((ri >> (sh + 1)) == (ci >> (sh + 1))) & (((ri >> sh) & 1) == 1) & (((ci >> sh) & 1) == 0))
        sh += 1

    units = [(c, h) for c in range(n_chunks) for h in range(DN_HEADS)]
    ge, kd, qd, rhs, qk, x = {}, {}, {}, {}, {}, {}
    for c in range(n_chunks):
        r0 = c * C
        g_all = gc_s[r0:r0 + C, :]
        g_last = gc_s[r0 + C - 1:r0 + C, :]
        eg_all = jnp.exp(g_all)
        ekd_all = jnp.exp(g_last - g_all)
        ge_all = jnp.exp(g_last)
        beta_all = bt_s[r0:r0 + C, :]
        gr_all = gr_s[c]
        for h in range(DN_HEADS):
            u_ = (c, h)
            sl = slice(h * DK, (h + 1) * DK)
            gl = DN_HEADS + h
            qh = q_s[r0:r0 + C, sl]
            kh = k_s[r0:r0 + C, sl]
            beta = beta_all[:, h:h + 1]
            eg = eg_all[:, gl:gl + 1]
            decay = jnp.exp(jnp.where(causal, g_all[:, gl:gl + 1] - gr_all[gl:gl + 1, :], -jnp.inf))
            kb = kh * beta
            both = _mm_nt(jnp.concatenate([kb, qh], axis=0), kh)
            x[u_] = jnp.where(strict, both[:C] * decay, 0.0)
            qk[u_] = jnp.where(causal, both[C:] * decay, 0.0).astype(BF16)
            rhs[u_] = jnp.concatenate([v_s[r0:r0 + C, sl] * beta, kb * eg], axis=1).astype(BF16)
            qd[u_] = (qh * eg).astype(BF16)
            kd[u_] = (kh * ekd_all[:, gl:gl + 1]).astype(BF16)
            ge[u_] = ge_all[:, gl:gl + 1]
    p = {u_: jnp.where(diag_blk, -x[u_], 0.0) for u_ in units}
    t = {u_: eye + p[u_] for u_ in units}
    for _ in range(2):
        p = {u_: _mm(p[u_], p[u_]) for u_ in units}
        t = {u_: t[u_] + _mm(t[u_], p[u_]) for u_ in units}
    for m in merge_masks:
        y = {u_: _mm(jnp.where(m, x[u_], 0.0), t[u_]) for u_ in units}
        t = {u_: t[u_] - _mm(t[u_], y[u_]) for u_ in units}
    sol = {u_: _mm(t[u_], rhs[u_]) for u_ in units}
    for c in range(n_chunks):
        r0 = c * C
        heads = [(c, h) for h in range(DN_HEADS)]
        st = {u_: st_s[u_[1]] for u_ in heads}
        ws = {u_: _mm(jnp.concatenate([sol[u_][:, DK:], qd[u_]], axis=0), st[u_]) for u_ in heads}
        v_new = {u_: (sol[u_][:, :DK] - ws[u_][:C]).astype(BF16) for u_ in heads}
        for u_ in heads:
            h = u_[1]
            o_s[r0:r0 + C, h * DK:(h + 1) * DK] = ws[u_][C:] + _mm(qk[u_], v_new[u_])
            st_s[h] = st[u_] * ge[u_] + _mm_tn(kd[u_], v_new[u_])

    for h in range(DN_HEADS):
        sl = slice(h * DK, (h + 1) * DK)
        oh = o_s[:, sl]
        y = oh * lax.rsqrt(jnp.mean(oh * oh, axis=-1, keepdims=True) + NORM_EPS) * nw_ref[...]
        o_ref[:, sl] = (y * _silu(z_ref[:, sl].astype(F32))).astype(BF16)


def _delta(proj, ab, abt, cw, alr, dtr, alc, dtc, nw, B, S, tm):
    T = proj.shape[0]
    HD = DN_HEADS * DN_DK
    spb = S // tm
    n_chunks = tm // DN_CHUNK
    col = lambda k: pl.BlockSpec((tm, HD), lambda b, j: (b * spb + j, k))
    const = lambda shape: pl.BlockSpec(shape, lambda b, j: (0,) * len(shape))
    return pl.pallas_call(
        _delta_kernel,
        grid=(B, spb),
        in_specs=[
            col(2), col(3), col(4), col(5),
            pl.BlockSpec((tm, LANES), lambda b, j: (b * spb + j, 0)),
            pl.BlockSpec((None, 2 * DN_HEADS, tm), lambda b, j: (b, 0, j)),
            const((CONV_WIDTH, 3 * HD)),
            const((1, LANES)), const((1, LANES)),
            const((2 * DN_HEADS, 1)), const((2 * DN_HEADS, 1)),
            const((1, DN_DK)),
        ],
        out_specs=pl.BlockSpec((tm, HD), lambda b, j: (b * spb + j, 0)),
        out_shape=jax.ShapeDtypeStruct((T, HD), BF16),
        scratch_shapes=[
            pltpu.VMEM((tm + HALO, HD), F32), pltpu.VMEM((tm + HALO, HD), F32), pltpu.VMEM((tm + HALO, HD), F32),
            pltpu.VMEM((tm, HD), F32), pltpu.VMEM((tm, HD), F32), pltpu.VMEM((tm, HD), F32), pltpu.VMEM((tm, HD), F32),
            pltpu.VMEM((tm, LANES), F32),
            pltpu.VMEM((n_chunks, 2 * DN_HEADS, DN_CHUNK), F32),
            pltpu.VMEM((tm, LANES), F32),
            pltpu.VMEM((DN_HEADS, DN_DK, DN_DK), F32),
        ],
        compiler_params=_params("arbitrary", "arbitrary"),
        name="delta",
    )(proj, proj, proj, proj, ab, abt, cw, alr, dtr, alc, dtc, nw)


def _pack_bf16_pairs(h):
    n = h.shape[1] // 2
    hb = h.astype(BF16).astype(F32)
    lo = lax.bitcast_convert_type(hb[:, :n], U32)
    hi = lax.bitcast_convert_type(hb[:, n:], U32)
    return hi | (lo >> 16)


def _unpack_bf16_pairs(p):
    lo = lax.bitcast_convert_type(p << 16, F32)
    hi = lax.bitcast_convert_type(p & jnp.uint32(0xFFFF0000), F32)
    return jnp.concatenate([lo, hi], axis=1).astype(BF16)


def _merge_kernel(yrg_ref, ydn_ref, grg_ref, gdn_ref, x_ref, mod_ref, wbr_ref, wbd_ref, wo_ref, nw_ref, wr_ref,
                  x1_ref, hp_ref, lg_ref):
    m = (_sigmoid(grg_ref[...].astype(F32)) * jnp.dot(yrg_ref[...], wbr_ref[...], preferred_element_type=F32)
         + _sigmoid(gdn_ref[...].astype(F32)) * jnp.dot(ydn_ref[...], wbd_ref[...], preferred_element_type=F32))
    x1 = x_ref[...] + mod_ref[2:3, :] * jnp.dot(m.astype(BF16), wo_ref[...], preferred_element_type=F32)
    x1_ref[...] = x1
    y = x1 * lax.rsqrt(jnp.mean(x1 * x1, axis=-1, keepdims=True) + NORM_EPS) * nw_ref[...]
    h2 = y * (1.0 + mod_ref[4:5, :]) + mod_ref[3:4, :]
    h_hi = h2.astype(BF16)
    h_lo = (h2 - h_hi.astype(F32)).astype(BF16)
    hh = jnp.dot(h_hi, wr_ref[...], preferred_element_type=F32)
    lg_ref[...] = hh[:, :LANES] + (hh[:, LANES:] + jnp.dot(h_lo, wr_ref[:, :LANES], preferred_element_type=F32))
    hp_ref[...] = _pack_bf16_pairs(h2)


def _merge(y_rg, y_dn, proj, x2, mod3, wbr, wbd, wo, nw, wr, S, tm):
    T, D = x2.shape
    spb = S // tm
    tile = lambda k=0: pl.BlockSpec((tm, D), lambda i: (i, k))
    const = lambda shape: pl.BlockSpec(shape, lambda i: (0,) * len(shape))
    return pl.pallas_call(
        _merge_kernel,
        grid=(T // tm,),
        in_specs=[
            tile(), tile(), tile(6), tile(7), tile(),
            pl.BlockSpec((None, 6, D), lambda i: (i // spb, 0, 0)),
            const((D, D)), const((D, D)), const((D, D)), const((1, D)), const((D, 2 * LANES)),
        ],
        out_specs=[tile(), pl.BlockSpec((tm, D // 2), lambda i: (i, 0)), pl.BlockSpec((tm, LANES), lambda i: (i, 0))],
        out_shape=[jax.ShapeDtypeStruct((T, D), F32), jax.ShapeDtypeStruct((T, D // 2), U32),
                   jax.ShapeDtypeStruct((T, LANES), F32)],
        compiler_params=_params("arbitrary"),
        name="merge",
    )(y_rg, y_dn, proj, proj, x2, mod3, wbr, wbd, wo, nw, wr)


def _route_kernel(lg_ref, bias_ref, slab_ref, slabt_ref, cnt_ref, carry):
    tm = lg_ref.shape[0]

    @pl.when(pl.program_id(0) == 0)
    def _():
        carry[...] = jnp.zeros_like(carry)

    lg = lg_ref[...] + bias_ref[...]
    lane = lax.broadcasted_iota(I32, (tm, LANES), 1)
    big = jnp.int32(LANES)
    ninf = -jnp.inf

    def first_max(vals):
        vmax = jnp.max(vals, axis=-1, keepdims=True)
        return vmax, jnp.min(jnp.where(vals == vmax, lane, big), axis=-1, keepdims=True)

    is_g = lane < N_GROUPS
    gmax, gsel = first_max(jnp.where(is_g, lg, ninf))
    p_g = 1.0 / jnp.sum(jnp.where(is_g, jnp.exp(lg - gmax), 0.0), axis=-1, keepdims=True)
    e_lane = lane - N_GROUPS
    in_group = (e_lane >= 0) & (e_lane < N_EXPERTS) & ((e_lane >> 3) == gsel)
    el = jnp.where(in_group, lg, ninf)
    v1, l1 = first_max(el)
    v2, l2 = first_max(jnp.where(lane == l1, ninf, el))
    ex = jnp.exp(v2 - v1)
    w1 = p_g / (1.0 + ex)
    w2 = p_g * ex / (1.0 + ex)
    e1 = l1 - N_GROUPS
    e2 = l2 - N_GROUPS

    hit1 = lane == e1
    hit2 = lane == e2
    onehot = jnp.where(hit1 | hit2, 1.0, 0.0)
    ri = lax.broadcasted_iota(I32, (tm, tm), 0)
    ci = lax.broadcasted_iota(I32, (tm, tm), 1)
    before = jnp.where(ri > ci, 1.0, 0.0).astype(BF16)
    cum = jnp.dot(before, onehot.astype(BF16), preferred_element_type=F32) + carry[...]
    r1 = jnp.sum(jnp.where(hit1, cum, 0.0), axis=-1, keepdims=True)
    r2 = jnp.sum(jnp.where(hit2, cum, 0.0), axis=-1, keepdims=True)
    carry[...] = carry[...] + jnp.sum(onehot, axis=0, keepdims=True)
    cnt_ref[...] = carry[...]

    slab = jnp.where(lane == 0, e1.astype(F32), 0.0)
    slab = jnp.where(lane == 1, e2.astype(F32), slab)
    slab = jnp.where(lane == 2, r1, slab)
    slab = jnp.where(lane == 3, r2, slab)
    slab = jnp.where(lane == 4, w1, slab)
    slab = jnp.where(lane == 5, w2, slab)
    slab_ref[...] = slab
    slabt_ref[...] = jnp.transpose(slab)[0:SUBLANES, :]


def _route(logits, bias, tm):
    T = logits.shape[0]
    return pl.pallas_call(
        _route_kernel,
        grid=(T // tm,),
        in_specs=[pl.BlockSpec((tm, LANES), lambda i: (i, 0)), pl.BlockSpec((1, LANES), lambda i: (0, 0))],
        out_specs=[pl.BlockSpec((tm, LANES), lambda i: (i, 0)), pl.BlockSpec((SUBLANES, tm), lambda i: (0, i)),
                   pl.BlockSpec((1, LANES), lambda i: (0, 0))],
        out_shape=[jax.ShapeDtypeStruct((T, LANES), F32), jax.ShapeDtypeStruct((SUBLANES, T), F32),
                   jax.ShapeDtypeStruct((1, LANES), F32)],
        scratch_shapes=[pltpu.VMEM((1, LANES), F32)],
        compiler_params=_params("arbitrary"),
        name="route",
    )(logits, bias)


def _row_copy(src, dst, s, d, sem):
    return pltpu.make_async_copy(src.at[pl.ds(s, 1), :], dst.at[pl.ds(d, 1), :], sem)


def _sort_kernel(dest_ref, hp_ref, xs_in_ref, xs_ref, sem, *, tb):
    del xs_in_ref

    def issue(r, carry):
        _row_copy(hp_ref, xs_ref, r, dest_ref[0, r], sem).start()
        _row_copy(hp_ref, xs_ref, r, dest_ref[0, tb + r], sem).start()
        return carry

    lax.fori_loop(0, tb, issue, 0, unroll=8)

    def drain(r, carry):
        _row_copy(hp_ref, xs_ref, 0, 0, sem).wait()
        _row_copy(hp_ref, xs_ref, 0, 0, sem).wait()
        return carry

    lax.fori_loop(0, tb, drain, 0, unroll=8)


def _sort(dest3, hp, xs0, tb):
    T, Dh = hp.shape
    return pl.pallas_call(
        functools.partial(_sort_kernel, tb=tb),
        grid=(T // tb,),
        in_specs=[
            pl.BlockSpec((None, 1, 2 * tb), lambda i: (i, 0, 0), memory_space=pltpu.SMEM),
            pl.BlockSpec((tb, Dh), lambda i: (i, 0)),
            pl.BlockSpec(memory_space=pl.ANY),
        ],
        out_specs=pl.BlockSpec(memory_space=pl.ANY),
        out_shape=jax.ShapeDtypeStruct(xs0.shape, xs0.dtype),
        scratch_shapes=[pltpu.SemaphoreType.DMA],
        input_output_aliases={2: 0},
        compiler_params=pltpu.CompilerParams(dimension_semantics=("arbitrary",), has_side_effects=True),
        name="sort",
    )(dest3, hp, xs0)


def _experts_kernel(be_ref, nv_ref, x_ref, wg_ref, wu_ref, wd_ref, y_ref, wg_b, wu_b, wd_b):
    i = pl.program_id(0)
    valid = i < nv_ref[0]

    @pl.when(valid & ((i == 0) | (be_ref[i] != be_ref[jnp.maximum(i - 1, 0)])))
    def _():
        wg_b[...] = wg_ref[...].astype(BF16)
        wu_b[...] = wu_ref[...].astype(BF16)
        wd_b[...] = wd_ref[...].astype(BF16)

    @pl.when(valid)
    def _():
        x = _unpack_bf16_pairs(x_ref[...])
        g = jnp.dot(x, wg_b[...], preferred_element_type=F32)
        u = jnp.dot(x, wu_b[...], preferred_element_type=F32)
        y_ref[...] = jnp.dot((_silu(g) * u).astype(BF16), wd_b[...], preferred_element_type=F32)

    @pl.when(jnp.logical_not(valid))
    def _():
        y_ref[...] = jnp.zeros_like(y_ref)


def _experts(blk_e, nvb, xs, wg, wu, wd):
    P, Dh = xs.shape
    E, D, DE = wg.shape
    NB = P // MOE_BLOCK
    last = lambda i, nv: jnp.minimum(i, nv[0] - 1)
    grid_spec = pltpu.PrefetchScalarGridSpec(
        num_scalar_prefetch=2,
        grid=(NB,),
        in_specs=[
            pl.BlockSpec((MOE_BLOCK, Dh), lambda i, be, nv: (last(i, nv), 0)),
            pl.BlockSpec((None, D, DE), lambda i, be, nv: (be[last(i, nv)], 0, 0)),
            pl.BlockSpec((None, D, DE), lambda i, be, nv: (be[last(i, nv)], 0, 0)),
            pl.BlockSpec((None, DE, D), lambda i, be, nv: (be[last(i, nv)], 0, 0)),
        ],
        out_specs=pl.BlockSpec((MOE_BLOCK, D), lambda i, be, nv: (i, 0)),
        scratch_shapes=[pltpu.VMEM((D, DE), BF16), pltpu.VMEM((D, DE), BF16), pltpu.VMEM((DE, D), BF16)],
    )
    return pl.pallas_call(
        _experts_kernel,
        grid_spec=grid_spec,
        out_shape=jax.ShapeDtypeStruct((P, D), F32),
        compiler_params=_params("arbitrary"),
        name="experts",
    )(blk_e, nvb, xs, wg, wu, wd)


def _final_kernel(dest_ref, x1_ref, slab_ref, mod_ref, fw_ref, ys_ref, o_ref, ybuf, sem, *, tb):
    def issue(r, carry):
        _row_copy(ys_ref, ybuf.at[0], dest_ref[0, r], r, sem).start()
        _row_copy(ys_ref, ybuf.at[1], dest_ref[0, tb + r], r, sem).start()
        return carry

    lax.fori_loop(0, tb, issue, 0, unroll=8)

    def drain(r, carry):
        _row_copy(ys_ref, ybuf.at[0], 0, 0, sem).wait()
        _row_copy(ys_ref, ybuf.at[1], 0, 0, sem).wait()
        return carry

    lax.fori_loop(0, tb, drain, 0, unroll=8)

    slab = slab_ref[...]
    y = slab[:, 4:5] * ybuf[0] + slab[:, 5:6] * ybuf[1]
    x2 = x1_ref[...] + mod_ref[5:6, :] * y
    o_ref[...] = x2 * lax.rsqrt(jnp.mean(x2 * x2, axis=-1, keepdims=True) + NORM_EPS) * fw_ref[...]


def _final(dest3, x1, slab, mod3, fw, ys, S, tb):
    T, D = x1.shape
    spb = S // tb
    return pl.pallas_call(
        functools.partial(_final_kernel, tb=tb),
        grid=(T // tb,),
        in_specs=[
            pl.BlockSpec((None, 1, 2 * tb), lambda i: (i, 0, 0), memory_space=pltpu.SMEM),
            pl.BlockSpec((tb, D), lambda i: (i, 0)),
            pl.BlockSpec((tb, LANES), lambda i: (i, 0)),
            pl.BlockSpec((None, 6, D), lambda i: (i // spb, 0, 0)),
            pl.BlockSpec((1, D), lambda i: (0, 0)),
            pl.BlockSpec(memory_space=pl.ANY),
        ],
        out_specs=pl.BlockSpec((tb, D), lambda i: (i, 0)),
        out_shape=jax.ShapeDtypeStruct((T, D), F32),
        scratch_shapes=[pltpu.VMEM((2, tb, D), F32), pltpu.SemaphoreType.DMA],
        compiler_params=_params("arbitrary"),
        name="final",
    )(dest3, x1, slab, mod3, fw, ys)


def _lane_row(vals, offset):
    return jnp.zeros((1, LANES), F32).at[0, offset:offset + vals.shape[0]].set(vals.astype(F32))


def _forward(x, c, w_ada, b_ada, norm1_w, w_in, rg_conv_w, rg_conv_b, rg_gate_a_w, rg_gate_a_b, rg_gate_x_w, rg_gate_x_b,
             rg_lambda, dn_conv_w, dn_a_log, dn_dt_bias, dn_norm_w, w_branch_rg, w_branch_dn, w_out, norm2_w,
             moe_w_group, moe_b_group, moe_w_router, moe_b_router, moe_w_gate, moe_w_up, moe_w_down, final_norm_w,
             tm_proj=512, tm_seq=256, tm_delta=256, tm_tok=512):
    B, S, D = x.shape
    T = B * S
    H = DN_HEADS
    x2 = x.reshape(T, D)
    mod3 = _ada(c, w_ada[0], b_ada[0]).reshape(B, 6, D)

    w = w_in[0]
    n_wide = 6 * D
    w_main = jnp.concatenate([w[:, :n_wide], w[:, n_wide + 2 * H:]], axis=1).astype(BF16)
    w_small = jnp.pad(w[:, n_wide:n_wide + 2 * H], ((0, 0), (0, LANES - 2 * H))).astype(BF16)
    proj, ab = _inproj(x2, mod3, norm1_w, w_main, w_small, S, tm_proj)

    y_rg = _rglru(proj, rg_conv_w[0], rg_conv_b, rg_gate_a_w[0].astype(BF16), rg_gate_a_b.reshape(1, -1),
                  rg_gate_x_w[0].astype(BF16), rg_gate_x_b.reshape(1, -1), rg_lambda, B, S, tm_seq)

    abt = ab[:, :2 * H].reshape(B, S, 2 * H).transpose(0, 2, 1)
    zeros_h = jnp.zeros((H,), F32)
    col = lambda v: jnp.concatenate([zeros_h, v.astype(F32)]).reshape(2 * H, 1)
    y_dn = _delta(proj, ab, abt, dn_conv_w[0], _lane_row(dn_a_log[0], H), _lane_row(dn_dt_bias[0], H),
                  col(dn_a_log[0]), col(dn_dt_bias[0]), dn_norm_w, B, S, tm_delta)

    w_route = jnp.pad(jnp.concatenate([moe_w_group[0], moe_w_router[0]], axis=1), ((0, 0), (0, LANES - N_GROUPS - N_EXPERTS)))
    w_route_hi = w_route.astype(BF16)
    w_route = jnp.concatenate([w_route_hi, (w_route - w_route_hi.astype(F32)).astype(BF16)], axis=1)
    x1, hp, logits = _merge(y_rg, y_dn, proj, x2, mod3, w_branch_rg[0].astype(BF16), w_branch_dn[0].astype(BF16),
                            w_out[0].astype(BF16), norm2_w, w_route, S, tm_proj)

    bias = _lane_row(jnp.concatenate([moe_b_group[0], moe_b_router[0]]), 0)
    slab, slabt, cnt = _route(logits, bias, tm_tok)

    eid = slabt[0:2].astype(I32)
    rank = slabt[2:4].astype(I32)
    counts = cnt[0, :N_EXPERTS].astype(I32)
    padded = (counts + MOE_BLOCK - 1) // MOE_BLOCK * MOE_BLOCK
    pad_end = jnp.cumsum(padded)
    pad_start = pad_end - padded
    seg = jnp.sum(jnp.where(eid[:, :, None] == jnp.arange(N_EXPERTS, dtype=I32), pad_start, 0), axis=-1)
    dest = seg + rank
    nt = T // tm_tok
    dest3 = dest.reshape(2, nt, tm_tok).transpose(1, 0, 2).reshape(nt, 1, 2 * tm_tok)
    A = 2 * T
    P = (A + N_EXPERTS * (MOE_BLOCK - 1) + MOE_BLOCK - 1) // MOE_BLOCK * MOE_BLOCK
    NB = P // MOE_BLOCK
    starts = jnp.arange(NB, dtype=I32) * MOE_BLOCK
    blk_e = jnp.minimum(jnp.sum(pad_end[None, :] <= starts[:, None], axis=1), N_EXPERTS - 1).astype(I32)
    nvb = (pad_end[-1:] // MOE_BLOCK).astype(I32)

    xs = _sort(dest3, hp, jnp.zeros((P, D // 2), U32), tm_tok)
    ys = _experts(blk_e, nvb, xs, moe_w_gate[0], moe_w_up[0], moe_w_down[0])
    out = _final(dest3, x1, slab, mod3, final_norm_w.reshape(1, D), ys, S, tm_tok)
    return out.reshape(B, S, D)


def kernel(x, c, w_ada, b_ada, norm1_w, w_in, rg_conv_w, rg_conv_b, rg_gate_a_w, rg_gate_a_b, rg_gate_x_w, rg_gate_x_b, rg_lambda, dn_conv_w, dn_a_log, dn_dt_bias, dn_norm_w, w_branch_rg, w_branch_dn, w_out, norm2_w, moe_w_group, moe_b_group, moe_w_router, moe_b_router, moe_w_gate, moe_w_up, moe_w_down, final_norm_w):
    return _forward(x, c, w_ada, b_ada, norm1_w, w_in, rg_conv_w, rg_conv_b, rg_gate_a_w, rg_gate_a_b, rg_gate_x_w, rg_gate_x_b, rg_lambda, dn_conv_w, dn_a_log, dn_dt_bias, dn_norm_w, w_branch_rg, w_branch_dn, w_out, norm2_w, moe_w_group, moe_b_group, moe_w_router, moe_b_router, moe_w_gate, moe_w_up, moe_w_down, final_norm_w)
```

```python
import functools

import jax
import jax.numpy as jnp
from jax import lax
from jax.experimental import pallas as pl
from jax.experimental.pallas import tpu as pltpu

F32 = jnp.float32
BF16 = jnp.bfloat16
U32 = jnp.uint32
I32 = jnp.int32
HIGHEST = lax.Precision.HIGHEST

NORM_EPS = 1e-6
RG_C = 8.0
RG_BLOCKS = 4
CONV_WIDTH = 4
DN_HEADS = 8
DN_DK = 128
DN_CHUNK = 64
N_GROUPS = 4
EXPERTS_PER_GROUP = 8
N_EXPERTS = N_GROUPS * EXPERTS_PER_GROUP
MOE_BLOCK = 512
LANES = 128
SUBLANES = 8
HALO = SUBLANES
VMEM_LIMIT = 48 * 1024 * 1024
VMEM_LIMIT_BIG = 56 * 1024 * 1024


def _sigmoid(x):
    return 1.0 / (1.0 + jnp.exp(-x))


def _silu(x):
    return x * _sigmoid(x)


def _softplus(x):
    return jnp.maximum(x, 0.0) + jnp.log(1.0 + jnp.exp(-jnp.abs(x)))


def _gelu_tanh(x):
    return x * (0.5 * (1.0 + jnp.tanh(0.7978845608028654 * (x + 0.044715 * (x * x * x)))))


def _mm(a, b):
    return jnp.dot(a.astype(BF16), b.astype(BF16), preferred_element_type=F32)


def _mm_nt(a, b):
    return lax.dot_general(a.astype(BF16), b.astype(BF16), (((1,), (1,)), ((), ())), preferred_element_type=F32)


def _mm_tn(a, b):
    return lax.dot_general(a.astype(BF16), b.astype(BF16), (((0,), (0,)), ((), ())), preferred_element_type=F32)


def _params(*sem, limit=None):
    return pltpu.CompilerParams(dimension_semantics=sem, vmem_limit_bytes=limit or VMEM_LIMIT)


def _ada_kernel(c_ref, w_ref, b_ref, o_ref):
    o_ref[...] = jnp.dot(_silu(c_ref[...]), w_ref[...], preferred_element_type=F32, precision=HIGHEST) + b_ref[...]


def _ada(c, w_ada, b_ada):
    B, D = c.shape
    N = w_ada.shape[1]
    rows = -(-B // SUBLANES) * SUBLANES
    cp = jnp.pad(c, ((0, rows - B), (0, 0)))
    tn = 1536
    out = pl.pallas_call(
        _ada_kernel,
        grid=(N // tn,),
        in_specs=[
            pl.BlockSpec((rows, D), lambda n: (0, 0)),
            pl.BlockSpec((D, tn), lambda n: (0, n)),
            pl.BlockSpec((1, tn), lambda n: (0, n)),
        ],
        out_specs=pl.BlockSpec((rows, tn), lambda n: (0, n)),
        out_shape=jax.ShapeDtypeStruct((rows, N), F32),
        compiler_params=_params("arbitrary"),
        name="ada",
    )(cp, w_ada, b_ada.reshape(1, N))
    return out[:B]


def _causal_conv(x, buf, w_ref, c0, width, tm):
    buf[HALO:HALO + tm, :] = x
    first = HALO - (CONV_WIDTH - 1)
    acc = w_ref[0:1, c0:c0 + width] * buf[first:first + tm, :]
    for j in range(1, CONV_WIDTH):
        acc = acc + w_ref[j:j + 1, c0:c0 + width] * buf[first + j:first + j + tm, :]
    buf[0:HALO, :] = buf[tm:tm + HALO, :]
    return acc


def _inproj_rg_kernel(x_ref, mod_ref, nw_ref, w_ref, ws_ref, cw_ref, cb_ref, wa_ref, ba_ref, wx_ref, bx_ref, lam_ref,
                      o_ref, os_ref, yrg_ref, xbuf, r_s, i_s, g_s, xc_s, xb_s, hb_s, h_s, *, spb):
    tm = x_ref.shape[0]
    W = cw_ref.shape[1]
    bd = W // RG_BLOCKS
    G = SUBLANES
    n_blocks = w_ref.shape[0]
    n_tail = n_blocks - 4

    @pl.when(pl.program_id(0) % spb == 0)
    def _():
        xbuf[0:HALO, :] = jnp.zeros((HALO, W), F32)
        h_s[...] = jnp.zeros_like(h_s)

    x = x_ref[...]
    y = x * lax.rsqrt(jnp.mean(x * x, axis=-1, keepdims=True) + NORM_EPS) * nw_ref[...]
    hb_s[...] = (y * (1.0 + mod_ref[1:2, :]) + mod_ref[0:1, :]).astype(BF16)

    def proj(n):
        return jnp.dot(hb_s[...], w_ref[n], preferred_element_type=F32)

    xbuf[HALO:HALO + tm, :] = proj(0)
    first = HALO - (CONV_WIDTH - 1)
    for r0 in range(0, tm, 2 * G):
        acc = cw_ref[0:1, :] * xbuf[first + r0:first + r0 + 2 * G, :]
        for j in range(1, CONV_WIDTH):
            acc = acc + cw_ref[j:j + 1, :] * xbuf[first + j + r0:first + j + r0 + 2 * G, :]
        acc = acc + cb_ref[...]
        xc_s[r0:r0 + 2 * G, :] = acc
        xb_s[r0:r0 + 2 * G, :] = acc.astype(BF16)
    xbuf[0:HALO, :] = xbuf[tm:tm + HALO, :]
    o_ref[0] = proj(2).astype(BF16)
    o_ref[1] = proj(3).astype(BF16)
    for g in range(RG_BLOCKS):
        r_s[:, g * bd:(g + 1) * bd] = jnp.dot(xb_s[:, g * bd:(g + 1) * bd], wa_ref[g], preferred_element_type=F32)
        i_s[:, g * bd:(g + 1) * bd] = jnp.dot(xb_s[:, g * bd:(g + 1) * bd], wx_ref[g], preferred_element_type=F32)
    g_s[...] = _gelu_tanh(proj(1))

    log_a_scale = (-RG_C) * _softplus(-lam_ref[...])
    row = lax.broadcasted_iota(I32, (G, W), 0)
    rows_per_trip = tm // n_tail

    def trip(k, h):
        o_ref[2 + k] = proj(4 + k).astype(BF16)
        for g in range(rows_per_trip // G):
            sl = pl.ds(pl.multiple_of(k * rows_per_trip + g * G, G), G)
            a = jnp.exp(log_a_scale * _sigmoid(r_s[sl, :] + ba_ref[...]))
            b = jnp.sqrt(1.0 - a * a) * (_sigmoid(i_s[sl, :] + bx_ref[...]) * xc_s[sl, :])
            s = 1
            while s < G:
                keep = row >= s
                b = jnp.where(keep, a * pltpu.roll(b, s, 0) + b, b)
                a = jnp.where(keep, a * pltpu.roll(a, s, 0), a)
                s *= 2
            hg = a * h + b
            r_s[sl, :] = hg
            h = jnp.broadcast_to(hg[G - 1:G, :], (G, W))
        return h

    h_s[...] = lax.fori_loop(0, n_tail, trip, h_s[...])
    os_ref[...] = jnp.dot(hb_s[...], ws_ref[...], preferred_element_type=F32)
    yrg_ref[...] = (r_s[...] * g_s[...]).astype(BF16)


def _inproj_rg(x2, mod3, norm_w, w_blocks, w_small, cw, cb, wa, ba, wx, bx, lam, S, tm):
    T, D = x2.shape
    n_blocks, _, W = w_blocks.shape
    spb = S // tm
    bd = W // RG_BLOCKS
    const = lambda shape: pl.BlockSpec(shape, lambda i: (0,) * len(shape))
    return pl.pallas_call(
        functools.partial(_inproj_rg_kernel, spb=spb),
        grid=(T // tm,),
        in_specs=[
            pl.BlockSpec((tm, D), lambda i: (i, 0)),
            pl.BlockSpec((None, 6, D), lambda i: (i // spb, 0, 0)),
            const((1, D)),
            pl.BlockSpec((n_blocks, D, W), lambda i: (0, 0, 0), pipeline_mode=pl.Buffered(1)),
            const((D, LANES)),
            const((CONV_WIDTH, W)), const((1, W)),
            const((RG_BLOCKS, bd, bd)), const((1, W)),
            const((RG_BLOCKS, bd, bd)), const((1, W)),
            const((1, W)),
        ],
        out_specs=[
            pl.BlockSpec((n_blocks - 2, tm, W), lambda i: (0, i, 0)),
            pl.BlockSpec((tm, LANES), lambda i: (i, 0)),
            pl.BlockSpec((tm, W), lambda i: (i, 0)),
        ],
        out_shape=[jax.ShapeDtypeStruct((n_blocks - 2, T, W), BF16), jax.ShapeDtypeStruct((T, LANES), F32),
                   jax.ShapeDtypeStruct((T, W), BF16)],
        scratch_shapes=[
            pltpu.VMEM((tm + HALO, W), F32),
            pltpu.VMEM((tm, W), F32),
            pltpu.VMEM((tm, W), F32),
            pltpu.VMEM((tm, W), F32),
            pltpu.VMEM((tm, W), F32),
            pltpu.VMEM((tm, W), BF16),
            pltpu.VMEM((tm, D), BF16),
            pltpu.VMEM((SUBLANES, W), F32),
        ],
        compiler_params=_params("arbitrary", limit=VMEM_LIMIT_BIG),
        name="inproj_rg",
    )(x2, mod3, norm_w, w_blocks, w_small, cw, cb, wa, ba, wx, bx, lam)


def _delta_kernel(q_ref, k_ref, v_ref, z_ref, ab_ref, abt_ref, cw_ref, alr_ref, dtr_ref, alc_ref, dtc_ref, nw_ref, o_ref,
                  qbuf, kbuf, vbuf, q_s, k_s, v_s, o_s, gc_s, gr_s, bt_s, st_s):
    tm, HD = q_ref.shape
    C = DN_CHUNK
    DK = DN_DK
    n_chunks = tm // C

    @pl.when(pl.program_id(1) == 0)
    def _():
        for buf in (qbuf, kbuf, vbuf):
            buf[0:HALO, :] = jnp.zeros((HALO, HD), F32)
        st_s[...] = jnp.zeros_like(st_s)

    q_s[...] = _silu(_causal_conv(q_ref[...].astype(F32), qbuf, cw_ref, 0, HD, tm))
    k_s[...] = _silu(_causal_conv(k_ref[...].astype(F32), kbuf, cw_ref, HD, HD, tm))
    v_s[...] = _silu(_causal_conv(v_ref[...].astype(F32), vbuf, cw_ref, 2 * HD, HD, tm))
    for h in range(DN_HEADS):
        sl = slice(h * DK, (h + 1) * DK)
        qh = q_s[:, sl]
        q_s[:, sl] = qh * lax.rsqrt(jnp.sum(qh * qh, axis=-1, keepdims=True) + NORM_EPS) * (DK ** -0.5)
        kh = k_s[:, sl]
        k_s[:, sl] = kh * lax.rsqrt(jnp.sum(kh * kh, axis=-1, keepdims=True) + NORM_EPS)

    ab = ab_ref[...]
    bt_s[...] = _sigmoid(ab)
    g_rows = -jnp.exp(alr_ref[...]) * _softplus(ab + dtr_ref[...])
    g_cols = -jnp.exp(alc_ref[...]) * _softplus(abt_ref[...] + dtc_ref[...])
    ri = lax.broadcasted_iota(I32, (C, C), 0)
    ci = lax.broadcasted_iota(I32, (C, C), 1)
    causal = ri >= ci
    strict = ri > ci
    lower = causal.astype(F32)
    upper = (ri <= ci).astype(F32)
    for c in range(n_chunks):
        gc_s[c * C:(c + 1) * C, :] = jnp.dot(lower, g_rows[c * C:(c + 1) * C, :], preferred_element_type=F32, precision=HIGHEST)
        gr_s[c] = jnp.dot(g_cols[:, c * C:(c + 1) * C], upper, preferred_element_type=F32, precision=HIGHEST)
    eye = (ri == ci).astype(F32)
    diag_blk = strict & ((ri >> 3) == (ci >> 3))
    merge_masks = []
    sh = 3
    while (1 << sh) < C:
        merge_masks.append(((ri >> (sh + 1)) == (ci >> (sh + 1))) & (((ri >> sh) & 1) == 1) & (((ci >> sh) & 1) == 0))
        sh += 1

    units = [(c, h) for c in range(n_chunks) for h in range(DN_HEADS)]
    ge, kd, qd, rhs, qk, x = {}, {}, {}, {}, {}, {}
    for c in range(n_chunks):
        r0 = c * C
        g_all = gc_s[r0:r0 + C, :]
        g_last = gc_s[r0 + C - 1:r0 + C, :]
        eg_all = jnp.exp(g_all)
        ekd_all = jnp.exp(g_last - g_all)
        ge_all = jnp.exp(g_last)
        beta_all = bt_s[r0:r0 + C, :]
        gr_all = gr_s[c]
        for h in range(DN_HEADS):
            u_ = (c, h)
            sl = slice(h * DK, (h + 1) * DK)
            gl = DN_HEADS + h
            qh = q_s[r0:r0 + C, sl]
            kh = k_s[r0:r0 + C, sl]
            beta = beta_all[:, h:h + 1]
            eg = eg_all[:, gl:gl + 1]
            decay = jnp.exp(jnp.where(causal, g_all[:, gl:gl + 1] - gr_all[gl:gl + 1, :], -jnp.inf))
            kb = kh * beta
            both = _mm_nt(jnp.concatenate([kb, qh], axis=0), kh)
            x[u_] = jnp.where(strict, both[:C] * decay, 0.0)
            qk[u_] = jnp.where(causal, both[C:] * decay, 0.0).astype(BF16)
            rhs[u_] = jnp.concatenate([v_s[r0:r0 + C, sl] * beta, kb * eg], axis=1).astype(BF16)
            qd[u_] = (qh * eg).astype(BF16)
            kd[u_] = (kh * ekd_all[:, gl:gl + 1]).astype(BF16)
            ge[u_] = ge_all[:, gl:gl + 1]
    p = {u_: jnp.where(diag_blk, -x[u_], 0.0) for u_ in units}
    t = {u_: eye + p[u_] for u_ in units}
    for _ in range(2):
        p = {u_: _mm(p[u_], p[u_]) for u_ in units}
        t = {u_: t[u_] + _mm(t[u_], p[u_]) for u_ in units}
    for m in merge_masks:
        y = {u_: _mm(jnp.where(m, x[u_], 0.0), t[u_]) for u_ in units}
        t = {u_: t[u_] - _mm(t[u_], y[u_]) for u_ in units}
    sol = {u_: _mm(t[u_], rhs[u_]) for u_ in units}
    for c in range(n_chunks):
        r0 = c * C
        heads = [(c, h) for h in range(DN_HEADS)]
        st = {u_: st_s[u_[1]] for u_ in heads}
        ws = {u_: _mm(jnp.concatenate([sol[u_][:, DK:], qd[u_]], axis=0), st[u_]) for u_ in heads}
        v_new = {u_: (sol[u_][:, :DK] - ws[u_][:C]).astype(BF16) for u_ in heads}
        for u_ in heads:
            h = u_[1]
            o_s[r0:r0 + C, h * DK:(h + 1) * DK] = ws[u_][C:] + _mm(qk[u_], v_new[u_])
            st_s[h] = st[u_] * ge[u_] + _mm_tn(kd[u_], v_new[u_])

    for h in range(DN_HEADS):
        sl = slice(h * DK, (h + 1) * DK)
        oh = o_s[:, sl]
        y = oh * lax.rsqrt(jnp.mean(oh * oh, axis=-1, keepdims=True) + NORM_EPS) * nw_ref[...]
        o_ref[:, sl] = (y * _silu(z_ref[:, sl].astype(F32))).astype(BF16)


def _delta(proj, ab, abt, cw, alr, dtr, alc, dtc, nw, B, S, tm):
    T = proj.shape[1]
    HD = DN_HEADS * DN_DK
    spb = S // tm
    n_chunks = tm // DN_CHUNK
    col = lambda k: pl.BlockSpec((None, tm, HD), lambda b, j: (k, b * spb + j, 0))
    const = lambda shape: pl.BlockSpec(shape, lambda b, j: (0,) * len(shape))
    return pl.pallas_call(
        _delta_kernel,
        grid=(B, spb),
        in_specs=[
            col(0), col(1), col(2), col(3),
            pl.BlockSpec((tm, LANES), lambda b, j: (b * spb + j, 0)),
            pl.BlockSpec((None, 2 * DN_HEADS, tm), lambda b, j: (b, 0, j)),
            const((CONV_WIDTH, 3 * HD)),
            const((1, LANES)), const((1, LANES)),
            const((2 * DN_HEADS, 1)), const((2 * DN_HEADS, 1)),
            const((1, DN_DK)),
        ],
        out_specs=pl.BlockSpec((tm, HD), lambda b, j: (b * spb + j, 0)),
        out_shape=jax.ShapeDtypeStruct((T, HD), BF16),
        scratch_shapes=[
            pltpu.VMEM((tm + HALO, HD), F32), pltpu.VMEM((tm + HALO, HD), F32), pltpu.VMEM((tm + HALO, HD), F32),
            pltpu.VMEM((tm, HD), F32), pltpu.VMEM((tm, HD), F32), pltpu.VMEM((tm, HD), F32), pltpu.VMEM((tm, HD), F32),
            pltpu.VMEM((tm, LANES), F32),
            pltpu.VMEM((n_chunks, 2 * DN_HEADS, DN_CHUNK), F32),
            pltpu.VMEM((tm, LANES), F32),
            pltpu.VMEM((DN_HEADS, DN_DK, DN_DK), F32),
        ],
        compiler_params=_params("arbitrary", "arbitrary"),
        name="delta",
    )(proj, proj, proj, proj, ab, abt, cw, alr, dtr, alc, dtc, nw)


def _pack_bf16_pairs(h):
    n = h.shape[1] // 2
    hb = h.astype(BF16).astype(F32)
    lo = lax.bitcast_convert_type(hb[:, :n], U32)
    hi = lax.bitcast_convert_type(hb[:, n:], U32)
    return hi | (lo >> 16)


def _unpack_bf16_pairs(p):
    lo = lax.bitcast_convert_type(p << 16, F32)
    hi = lax.bitcast_convert_type(p & jnp.uint32(0xFFFF0000), F32)
    return jnp.concatenate([lo, hi], axis=1).astype(BF16)


def _merge_kernel(yrg_ref, ydn_ref, grg_ref, gdn_ref, x_ref, mod_ref, wbr_ref, wbd_ref, wo_ref, nw_ref, wr_ref, bias_ref,
                  x1_ref, hp_ref, slab_ref, slabt_ref, cnt_ref, carry, lg_s):
    i = pl.program_id(0)

    @pl.when(i == 0)
    def _():
        carry[...] = jnp.zeros_like(carry)
        lg_s[...] = jnp.zeros_like(lg_s)

    slab = _route_tile(lg_s[...], carry, jnp.where(i > 0, 1.0, 0.0))
    slab_ref[...] = slab
    slabt_ref[...] = jnp.transpose(slab)[0:SUBLANES, :]
    cnt_ref[...] = carry[...]

    m = (_sigmoid(grg_ref[...].astype(F32)) * jnp.dot(yrg_ref[...], wbr_ref[...], preferred_element_type=F32)
         + _sigmoid(gdn_ref[...].astype(F32)) * jnp.dot(ydn_ref[...], wbd_ref[...], preferred_element_type=F32))
    x1 = x_ref[...] + mod_ref[2:3, :] * jnp.dot(m.astype(BF16), wo_ref[...], preferred_element_type=F32)
    x1_ref[...] = x1
    y = x1 * lax.rsqrt(jnp.mean(x1 * x1, axis=-1, keepdims=True) + NORM_EPS) * nw_ref[...]
    h2 = y * (1.0 + mod_ref[4:5, :]) + mod_ref[3:4, :]
    h_hi = h2.astype(BF16)
    h_lo = (h2 - h_hi.astype(F32)).astype(BF16)
    hh = jnp.dot(h_hi, wr_ref[...], preferred_element_type=F32)
    lg_s[...] = (hh[:, :LANES] + (hh[:, LANES:] + jnp.dot(h_lo, wr_ref[:, :LANES], preferred_element_type=F32))) + bias_ref[...]
    hp_ref[...] = _pack_bf16_pairs(h2)


def _merge(y_rg, y_dn, proj, x2, mod3, wbr, wbd, wo, nw, wr, bias, S, tm):
    T, D = x2.shape
    spb = S // tm
    nt = T // tm
    cur = lambda i: jnp.minimum(i, nt - 1)
    prev = lambda i: jnp.maximum(i - 1, 0)
    tile = lambda: pl.BlockSpec((tm, D), lambda i: (cur(i), 0))
    blk = lambda k: pl.BlockSpec((None, tm, D), lambda i: (k, cur(i), 0))
    const = lambda shape: pl.BlockSpec(shape, lambda i: (0,) * len(shape))
    return pl.pallas_call(
        _merge_kernel,
        grid=(nt + 1,),
        in_specs=[
            tile(), tile(), blk(4), blk(5), tile(),
            pl.BlockSpec((None, 6, D), lambda i: (cur(i) // spb, 0, 0)),
            const((D, D)), const((D, D)), const((D, D)), const((1, D)), const((D, 2 * LANES)), const((1, LANES)),
        ],
        out_specs=[tile(), pl.BlockSpec((tm, D // 2), lambda i: (cur(i), 0)),
                   pl.BlockSpec((tm, LANES), lambda i: (prev(i), 0)),
                   pl.BlockSpec((SUBLANES, tm), lambda i: (0, prev(i))), const((1, LANES))],
        out_shape=[jax.ShapeDtypeStruct((T, D), F32), jax.ShapeDtypeStruct((T, D // 2), U32),
                   jax.ShapeDtypeStruct((T, LANES), F32), jax.ShapeDtypeStruct((SUBLANES, T), F32),
                   jax.ShapeDtypeStruct((1, LANES), F32)],
        scratch_shapes=[pltpu.VMEM((1, LANES), F32), pltpu.VMEM((tm, LANES), F32)],
        compiler_params=_params("arbitrary"),
        name="merge",
    )(y_rg, y_dn, proj, proj, x2, mod3, wbr, wbd, wo, nw, wr, bias)


def _route_tile(lg, carry, live):
    tm = lg.shape[0]
    lane = lax.broadcasted_iota(I32, (tm, LANES), 1)
    big = jnp.int32(LANES)
    ninf = -jnp.inf

    def first_max(vals):
        vmax = jnp.max(vals, axis=-1, keepdims=True)
        return vmax, jnp.min(jnp.where(vals == vmax, lane, big), axis=-1, keepdims=True)

    is_g = lane < N_GROUPS
    gmax, gsel = first_max(jnp.where(is_g, lg, ninf))
    p_g = 1.0 / jnp.sum(jnp.where(is_g, jnp.exp(lg - gmax), 0.0), axis=-1, keepdims=True)
    e_lane = lane - N_GROUPS
    in_group = (e_lane >= 0) & (e_lane < N_EXPERTS) & ((e_lane >> 3) == gsel)
    el = jnp.where(in_group, lg, ninf)
    v1, l1 = first_max(el)
    v2, l2 = first_max(jnp.where(lane == l1, ninf, el))
    ex = jnp.exp(v2 - v1)
    w1 = p_g / (1.0 + ex)
    w2 = p_g * ex / (1.0 + ex)
    e1 = l1 - N_GROUPS
    e2 = l2 - N_GROUPS

    hit1 = lane == e1
    hit2 = lane == e2
    onehot = jnp.where(hit1 | hit2, 1.0, 0.0)
    ri = lax.broadcasted_iota(I32, (tm, tm), 0)
    ci = lax.broadcasted_iota(I32, (tm, tm), 1)
    before = jnp.where(ri > ci, 1.0, 0.0).astype(BF16)
    cum = jnp.dot(before, onehot.astype(BF16), preferred_element_type=F32) + carry[...]
    r1 = jnp.sum(jnp.where(hit1, cum, 0.0), axis=-1, keepdims=True)
    r2 = jnp.sum(jnp.where(hit2, cum, 0.0), axis=-1, keepdims=True)
    carry[...] = carry[...] + live * jnp.sum(onehot, axis=0, keepdims=True)

    slab = jnp.where(lane == 0, e1.astype(F32), 0.0)
    slab = jnp.where(lane == 1, e2.astype(F32), slab)
    slab = jnp.where(lane == 2, r1, slab)
    slab = jnp.where(lane == 3, r2, slab)
    slab = jnp.where(lane == 4, w1, slab)
    return jnp.where(lane == 5, w2, slab)


def _row_copy(src, dst, s, d, sem):
    return pltpu.make_async_copy(src.at[pl.ds(s, 1), :], dst.at[pl.ds(d, 1), :], sem)


def _sort_kernel(dest_ref, hp_ref, xs_in_ref, xs_ref, sem, *, tb):
    del xs_in_ref

    def issue(r, carry):
        _row_copy(hp_ref, xs_ref, r, dest_ref[0, r], sem).start()
        _row_copy(hp_ref, xs_ref, r, dest_ref[0, tb + r], sem).start()
        return carry

    lax.fori_loop(0, tb, issue, 0, unroll=8)

    def drain(r, carry):
        _row_copy(hp_ref, xs_ref, 0, 0, sem).wait()
        _row_copy(hp_ref, xs_ref, 0, 0, sem).wait()
        return carry

    lax.fori_loop(0, tb, drain, 0, unroll=8)


def _sort(dest3, hp, xs0, tb):
    T, Dh = hp.shape
    return pl.pallas_call(
        functools.partial(_sort_kernel, tb=tb),
        grid=(T // tb,),
        in_specs=[
            pl.BlockSpec((None, 1, 2 * tb), lambda i: (i, 0, 0), memory_space=pltpu.SMEM),
            pl.BlockSpec((tb, Dh), lambda i: (i, 0)),
            pl.BlockSpec(memory_space=pl.ANY),
        ],
        out_specs=pl.BlockSpec(memory_space=pl.ANY),
        out_shape=jax.ShapeDtypeStruct(xs0.shape, xs0.dtype),
        scratch_shapes=[pltpu.SemaphoreType.DMA],
        input_output_aliases={2: 0},
        compiler_params=pltpu.CompilerParams(dimension_semantics=("arbitrary",), has_side_effects=True),
        name="sort",
    )(dest3, hp, xs0)


def _experts_kernel(be_ref, nv_ref, x_ref, wg_ref, wu_ref, wd_ref, y_ref, wg_b, wu_b, wd_b):
    i = pl.program_id(0)
    valid = i < nv_ref[0]

    @pl.when(valid & ((i == 0) | (be_ref[i] != be_ref[jnp.maximum(i - 1, 0)])))
    def _():
        wg_b[...] = wg_ref[...].astype(BF16)
        wu_b[...] = wu_ref[...].astype(BF16)
        wd_b[...] = wd_ref[...].astype(BF16)

    @pl.when(valid)
    def _():
        x = _unpack_bf16_pairs(x_ref[...])
        g = jnp.dot(x, wg_b[...], preferred_element_type=F32)
        u = jnp.dot(x, wu_b[...], preferred_element_type=F32)
        y_ref[...] = jnp.dot((_silu(g) * u).astype(BF16), wd_b[...], preferred_element_type=F32)

    @pl.when(jnp.logical_not(valid))
    def _():
        y_ref[...] = jnp.zeros_like(y_ref)


def _experts(blk_e, nvb, xs, wg, wu, wd):
    P, Dh = xs.shape
    E, D, DE = wg.shape
    NB = P // MOE_BLOCK
    last = lambda i, nv: jnp.minimum(i, nv[0] - 1)
    grid_spec = pltpu.PrefetchScalarGridSpec(
        num_scalar_prefetch=2,
        grid=(NB,),
        in_specs=[
            pl.BlockSpec((MOE_BLOCK, Dh), lambda i, be, nv: (last(i, nv), 0)),
            pl.BlockSpec((None, D, DE), lambda i, be, nv: (be[last(i, nv)], 0, 0)),
            pl.BlockSpec((None, D, DE), lambda i, be, nv: (be[last(i, nv)], 0, 0)),
            pl.BlockSpec((None, DE, D), lambda i, be, nv: (be[last(i, nv)], 0, 0)),
        ],
        out_specs=pl.BlockSpec((MOE_BLOCK, D), lambda i, be, nv: (i, 0)),
        scratch_shapes=[pltpu.VMEM((D, DE), BF16), pltpu.VMEM((D, DE), BF16), pltpu.VMEM((DE, D), BF16)],
    )
    return pl.pallas_call(
        _experts_kernel,
        grid_spec=grid_spec,
        out_shape=jax.ShapeDtypeStruct((P, D), F32),
        compiler_params=_params("arbitrary"),
        name="experts",
    )(blk_e, nvb, xs, wg, wu, wd)


def _final_kernel(dest_ref, x1_ref, slab_ref, mod_ref, fw_ref, ys_ref, o_ref, ybuf, sem, *, tb):
    def issue(r, carry):
        _row_copy(ys_ref, ybuf.at[0], dest_ref[0, r], r, sem).start()
        _row_copy(ys_ref, ybuf.at[1], dest_ref[0, tb + r], r, sem).start()
        return carry

    lax.fori_loop(0, tb, issue, 0, unroll=8)

    def drain(r, carry):
        _row_copy(ys_ref, ybuf.at[0], 0, 0, sem).wait()
        _row_copy(ys_ref, ybuf.at[1], 0, 0, sem).wait()
        return carry

    lax.fori_loop(0, tb, drain, 0, unroll=8)

    slab = slab_ref[...]
    y = slab[:, 4:5] * ybuf[0] + slab[:, 5:6] * ybuf[1]
    x2 = x1_ref[...] + mod_ref[5:6, :] * y
    o_ref[...] = x2 * lax.rsqrt(jnp.mean(x2 * x2, axis=-1, keepdims=True) + NORM_EPS) * fw_ref[...]


def _final(dest3, x1, slab, mod3, fw, ys, S, tb):
    T, D = x1.shape
    spb = S // tb
    return pl.pallas_call(
        functools.partial(_final_kernel, tb=tb),
        grid=(T // tb,),
        in_specs=[
            pl.BlockSpec((None, 1, 2 * tb), lambda i: (i, 0, 0), memory_space=pltpu.SMEM),
            pl.BlockSpec((tb, D), lambda i: (i, 0)),
            pl.BlockSpec((tb, LANES), lambda i: (i, 0)),
            pl.BlockSpec((None, 6, D), lambda i: (i // spb, 0, 0)),
            pl.BlockSpec((1, D), lambda i: (0, 0)),
            pl.BlockSpec(memory_space=pl.ANY),
        ],
        out_specs=pl.BlockSpec((tb, D), lambda i: (i, 0)),
        out_shape=jax.ShapeDtypeStruct((T, D), F32),
        scratch_shapes=[pltpu.VMEM((2, tb, D), F32), pltpu.SemaphoreType.DMA],
        compiler_params=_params("arbitrary"),
        name="final",
    )(dest3, x1, slab, mod3, fw, ys)


def _lane_row(vals, offset):
    return jnp.zeros((1, LANES), F32).at[0, offset:offset + vals.shape[0]].set(vals.astype(F32))


def _forward(x, c, w_ada, b_ada, norm1_w, w_in, rg_conv_w, rg_conv_b, rg_gate_a_w, rg_gate_a_b, rg_gate_x_w, rg_gate_x_b,
             rg_lambda, dn_conv_w, dn_a_log, dn_dt_bias, dn_norm_w, w_branch_rg, w_branch_dn, w_out, norm2_w,
             moe_w_group, moe_b_group, moe_w_router, moe_b_router, moe_w_gate, moe_w_up, moe_w_down, final_norm_w,
             tm_proj=512, tm_delta=256, tm_tok=512):
    B, S, D = x.shape
    T = B * S
    H = DN_HEADS
    x2 = x.reshape(T, D)
    mod3 = _ada(c, w_ada[0], b_ada[0]).reshape(B, 6, D)

    w = w_in[0]
    n_wide = 6 * D
    w_main = jnp.concatenate([w[:, :n_wide], w[:, n_wide + 2 * H:]], axis=1).astype(BF16)
    w_blocks = w_main.reshape(D, -1, D).transpose(1, 0, 2)
    w_small = jnp.pad(w[:, n_wide:n_wide + 2 * H], ((0, 0), (0, LANES - 2 * H))).astype(BF16)
    proj, ab, y_rg = _inproj_rg(x2, mod3, norm1_w, w_blocks, w_small, rg_conv_w[0], rg_conv_b,
                                rg_gate_a_w[0].astype(BF16), rg_gate_a_b.reshape(1, -1),
                                rg_gate_x_w[0].astype(BF16), rg_gate_x_b.reshape(1, -1), rg_lambda, S, tm_proj)

    abt = ab[:, :2 * H].reshape(B, S, 2 * H).transpose(0, 2, 1)
    zeros_h = jnp.zeros((H,), F32)
    col = lambda v: jnp.concatenate([zeros_h, v.astype(F32)]).reshape(2 * H, 1)
    y_dn = _delta(proj, ab, abt, dn_conv_w[0], _lane_row(dn_a_log[0], H), _lane_row(dn_dt_bias[0], H),
                  col(dn_a_log[0]), col(dn_dt_bias[0]), dn_norm_w, B, S, tm_delta)

    w_route = jnp.pad(jnp.concatenate([moe_w_group[0], moe_w_router[0]], axis=1), ((0, 0), (0, LANES - N_GROUPS - N_EXPERTS)))
    w_route_hi = w_route.astype(BF16)
    w_route = jnp.concatenate([w_route_hi, (w_route - w_route_hi.astype(F32)).astype(BF16)], axis=1)
    bias = _lane_row(jnp.concatenate([moe_b_group[0], moe_b_router[0]]), 0)
    x1, hp, slab, slabt, cnt = _merge(y_rg, y_dn, proj, x2, mod3, w_branch_rg[0].astype(BF16),
                                      w_branch_dn[0].astype(BF16), w_out[0].astype(BF16), norm2_w, w_route, bias, S, tm_proj)

    eid = slabt[0:2].astype(I32)
    rank = slabt[2:4].astype(I32)
    counts = cnt[0, :N_EXPERTS].astype(I32)
    padded = (counts + MOE_BLOCK - 1) // MOE_BLOCK * MOE_BLOCK
    pad_end = jnp.cumsum(padded)
    pad_start = pad_end - padded
    seg = jnp.sum(jnp.where(eid[:, :, None] == jnp.arange(N_EXPERTS, dtype=I32), pad_start, 0), axis=-1)
    dest = seg + rank
    nt = T // tm_tok
    dest3 = dest.reshape(2, nt, tm_tok).transpose(1, 0, 2).reshape(nt, 1, 2 * tm_tok)
    A = 2 * T
    P = (A + N_EXPERTS * (MOE_BLOCK - 1) + MOE_BLOCK - 1) // MOE_BLOCK * MOE_BLOCK
    NB = P // MOE_BLOCK
    starts = jnp.arange(NB, dtype=I32) * MOE_BLOCK
    blk_e = jnp.minimum(jnp.sum(pad_end[None, :] <= starts[:, None], axis=1), N_EXPERTS - 1).astype(I32)
    nvb = (pad_end[-1:] // MOE_BLOCK).astype(I32)

    xs = _sort(dest3, hp, jnp.zeros((P, D // 2), U32), tm_tok)
    ys = _experts(blk_e, nvb, xs, moe_w_gate[0], moe_w_up[0], moe_w_down[0])
    out = _final(dest3, x1, slab, mod3, final_norm_w.reshape(1, D), ys, S, tm_tok)
    return out.reshape(B, S, D)


def kernel(x, c, w_ada, b_ada, norm1_w, w_in, rg_conv_w, rg_conv_b, rg_gate_a_w, rg_gate_a_b, rg_gate_x_w, rg_gate_x_b, rg_lambda, dn_conv_w, dn_a_log, dn_dt_bias, dn_norm_w, w_branch_rg, w_branch_dn, w_out, norm2_w, moe_w_group, moe_b_group, moe_w_router, moe_b_router, moe_w_gate, moe_w_up, moe_w_down, final_norm_w):
    return _forward(x, c, w_ada, b_ada, norm1_w, w_in, rg_conv_w, rg_conv_b, rg_gate_a_w, rg_gate_a_b, rg_gate_x_w, rg_gate_x_b, rg_lambda, dn_conv_w, dn_a_log, dn_dt_bias, dn_norm_w, w_branch_rg, w_branch_dn, w_out, norm2_w, moe_w_group, moe_b_group, moe_w_router, moe_b_router, moe_w_gate, moe_w_up, moe_w_down, final_norm_w)
```

```python
import functools

import jax
import jax.numpy as jnp
from jax import lax
from jax.experimental import pallas as pl
from jax.experimental.pallas import tpu as pltpu

F32 = jnp.float32
BF16 = jnp.bfloat16
U32 = jnp.uint32
I32 = jnp.int32
HIGHEST = lax.Precision.HIGHEST

NORM_EPS = 1e-6
RG_C = 8.0
RG_BLOCKS = 4
CONV_WIDTH = 4
DN_HEADS = 8
DN_DK = 128
DN_CHUNK = 64
N_GROUPS = 4
EXPERTS_PER_GROUP = 8
N_EXPERTS = N_GROUPS * EXPERTS_PER_GROUP
MOE_BLOCK = 512
LANES = 128
SUBLANES = 8
HALO = SUBLANES
VMEM_LIMIT = 48 * 1024 * 1024
VMEM_LIMIT_BIG = 56 * 1024 * 1024


def _sigmoid(x):
    return 1.0 / (1.0 + jnp.exp(-x))


def _silu(x):
    return x * _sigmoid(x)


def _softplus(x):
    return jnp.maximum(x, 0.0) + jnp.log(1.0 + jnp.exp(-jnp.abs(x)))


def _gelu_tanh(x):
    return x * (0.5 * (1.0 + jnp.tanh(0.7978845608028654 * (x + 0.044715 * (x * x * x)))))


def _mm(a, b):
    return jnp.dot(a.astype(BF16), b.astype(BF16), preferred_element_type=F32)


def _mm_nt(a, b):
    return lax.dot_general(a.astype(BF16), b.astype(BF16), (((1,), (1,)), ((), ())), preferred_element_type=F32)


def _mm_tn(a, b):
    return lax.dot_general(a.astype(BF16), b.astype(BF16), (((0,), (0,)), ((), ())), preferred_element_type=F32)


def _params(*sem, limit=None):
    return pltpu.CompilerParams(dimension_semantics=sem, vmem_limit_bytes=limit or VMEM_LIMIT)


def _ada_kernel(c_ref, w_ref, b_ref, o_ref):
    o_ref[...] = jnp.dot(_silu(c_ref[...]), w_ref[...], preferred_element_type=F32, precision=HIGHEST) + b_ref[...]


def _ada(c, w_ada, b_ada):
    B, D = c.shape
    N = w_ada.shape[1]
    rows = -(-B // SUBLANES) * SUBLANES
    cp = jnp.pad(c, ((0, rows - B), (0, 0)))
    tn = 1536
    out = pl.pallas_call(
        _ada_kernel,
        grid=(N // tn,),
        in_specs=[
            pl.BlockSpec((rows, D), lambda n: (0, 0)),
            pl.BlockSpec((D, tn), lambda n: (0, n)),
            pl.BlockSpec((1, tn), lambda n: (0, n)),
        ],
        out_specs=pl.BlockSpec((rows, tn), lambda n: (0, n)),
        out_shape=jax.ShapeDtypeStruct((rows, N), F32),
        compiler_params=_params("arbitrary"),
        name="ada",
    )(cp, w_ada, b_ada.reshape(1, N))
    return out[:B]


def _causal_conv(x, buf, w_ref, c0, width, tm):
    buf[HALO:HALO + tm, :] = x
    first = HALO - (CONV_WIDTH - 1)
    acc = w_ref[0:1, c0:c0 + width] * buf[first:first + tm, :]
    for j in range(1, CONV_WIDTH):
        acc = acc + w_ref[j:j + 1, c0:c0 + width] * buf[first + j:first + j + tm, :]
    buf[0:HALO, :] = buf[tm:tm + HALO, :]
    return acc


def _inproj_rg_kernel(x_ref, mod_ref, nw_ref, w_ref, ws_ref, cw_ref, cb_ref, wa_ref, ba_ref, wx_ref, bx_ref, lam_ref,
                      o_ref, os_ref, yrg_ref, xbuf, r_s, i_s, g_s, xc_s, xb_s, hb_s, h_s, *, spb):
    tm = x_ref.shape[0]
    W = cw_ref.shape[2]
    bd = W // RG_BLOCKS
    G = SUBLANES
    n_blocks = w_ref.shape[0]
    n_tail = n_blocks - 4

    @pl.when(pl.program_id(0) % spb == 0)
    def _():
        xbuf[0:HALO, :] = jnp.zeros((HALO, W), F32)
        h_s[...] = jnp.zeros_like(h_s)

    x = x_ref[...]
    y = x * lax.rsqrt(jnp.mean(x * x, axis=-1, keepdims=True) + NORM_EPS) * nw_ref[...]
    hb_s[...] = (y * (1.0 + mod_ref[1:2, :]) + mod_ref[0:1, :]).astype(BF16)

    def proj(n):
        return jnp.dot(hb_s[...], w_ref[n], preferred_element_type=F32)

    xbuf[HALO:HALO + tm, :] = proj(0)
    first = HALO - (CONV_WIDTH - 1)
    for r0 in range(0, tm, 2 * G):
        acc = cw_ref[0] * xbuf[first + r0:first + r0 + 2 * G, :]
        for j in range(1, CONV_WIDTH):
            acc = acc + cw_ref[j] * xbuf[first + j + r0:first + j + r0 + 2 * G, :]
        acc = acc + cb_ref[...]
        xc_s[r0:r0 + 2 * G, :] = acc
        xb_s[r0:r0 + 2 * G, :] = acc.astype(BF16)
    xbuf[0:HALO, :] = xbuf[tm:tm + HALO, :]
    o_ref[0] = proj(2).astype(BF16)
    o_ref[1] = proj(3).astype(BF16)
    for g in range(RG_BLOCKS):
        r_s[:, g * bd:(g + 1) * bd] = jnp.dot(xb_s[:, g * bd:(g + 1) * bd], wa_ref[g], preferred_element_type=F32)
        i_s[:, g * bd:(g + 1) * bd] = jnp.dot(xb_s[:, g * bd:(g + 1) * bd], wx_ref[g], preferred_element_type=F32)
    g_s[...] = _gelu_tanh(proj(1))

    log_a_scale = (-RG_C) * _softplus(-lam_ref[...])
    row = lax.broadcasted_iota(I32, (G, W), 0)
    rows_per_trip = tm // n_tail

    def trip(k, h):
        o_ref[2 + k] = proj(4 + k).astype(BF16)
        for g in range(rows_per_trip // G):
            sl = pl.ds(pl.multiple_of(k * rows_per_trip + g * G, G), G)
            a = jnp.exp(log_a_scale * _sigmoid(r_s[sl, :] + ba_ref[...]))
            b = jnp.sqrt(1.0 - a * a) * (_sigmoid(i_s[sl, :] + bx_ref[...]) * xc_s[sl, :])
            s = 1
            while s < G:
                keep = row >= s
                b = jnp.where(keep, a * pltpu.roll(b, s, 0) + b, b)
                a = jnp.where(keep, a * pltpu.roll(a, s, 0), a)
                s *= 2
            hg = a * h + b
            r_s[sl, :] = hg
            h = jnp.broadcast_to(hg[G - 1:G, :], (G, W))
        return h

    h_s[...] = lax.fori_loop(0, n_tail, trip, h_s[...])
    os_ref[...] = jnp.dot(hb_s[...], ws_ref[...], preferred_element_type=F32)
    yrg_ref[...] = (r_s[...] * g_s[...]).astype(BF16)


def _inproj_rg(x2, mod3, norm_w, w_blocks, w_small, cw, cb, wa, ba, wx, bx, lam, S, tm):
    T, D = x2.shape
    n_blocks, _, W = w_blocks.shape
    spb = S // tm
    bd = W // RG_BLOCKS
    rep = lambda v, n: jnp.broadcast_to(v[..., None, :], v.shape[:-1] + (n, W))
    cw, cb = rep(cw, 2 * SUBLANES), rep(cb, 2 * SUBLANES)
    ba, bx, lam = rep(ba, SUBLANES), rep(bx, SUBLANES), rep(lam, SUBLANES)
    const = lambda shape: pl.BlockSpec(shape, lambda i: (0,) * len(shape))
    return pl.pallas_call(
        functools.partial(_inproj_rg_kernel, spb=spb),
        grid=(T // tm,),
        in_specs=[
            pl.BlockSpec((tm, D), lambda i: (i, 0)),
            pl.BlockSpec((None, 6, D), lambda i: (i // spb, 0, 0)),
            const((1, D)),
            pl.BlockSpec((n_blocks, D, W), lambda i: (0, 0, 0), pipeline_mode=pl.Buffered(1)),
            const((D, LANES)),
            const((CONV_WIDTH, 2 * SUBLANES, W)), const((2 * SUBLANES, W)),
            const((RG_BLOCKS, bd, bd)), const((SUBLANES, W)),
            const((RG_BLOCKS, bd, bd)), const((SUBLANES, W)),
            const((SUBLANES, W)),
        ],
        out_specs=[
            pl.BlockSpec((n_blocks - 2, tm, W), lambda i: (0, i, 0)),
            pl.BlockSpec((tm, LANES), lambda i: (i, 0)),
            pl.BlockSpec((tm, W), lambda i: (i, 0)),
        ],
        out_shape=[jax.ShapeDtypeStruct((n_blocks - 2, T, W), BF16), jax.ShapeDtypeStruct((T, LANES), F32),
                   jax.ShapeDtypeStruct((T, W), BF16)],
        scratch_shapes=[
            pltpu.VMEM((tm + HALO, W), F32),
            pltpu.VMEM((tm, W), F32),
            pltpu.VMEM((tm, W), F32),
            pltpu.VMEM((tm, W), F32),
            pltpu.VMEM((tm, W), F32),
            pltpu.VMEM((tm, W), BF16),
            pltpu.VMEM((tm, D), BF16),
            pltpu.VMEM((SUBLANES, W), F32),
        ],
        compiler_params=_params("arbitrary", limit=VMEM_LIMIT_BIG),
        name="inproj_rg",
    )(x2, mod3, norm_w, w_blocks, w_small, cw, cb, wa, ba, wx, bx, lam)


def _causal_conv_bf16(x_ref, halo, w_ref, c0, shifts_ref):
    tm, width = x_ref.shape
    xb = x_ref[...]
    xf = xb.astype(F32)
    w = lambda j: w_ref[j:j + 1, c0:c0 + width]
    shifted = jnp.dot(shifts_ref[...], xb, preferred_element_type=F32)
    acc = w(0) * shifted[0:tm]
    for j in range(1, CONV_WIDTH - 1):
        acc = acc + w(j) * shifted[j * tm:(j + 1) * tm]
    acc = acc + w(CONV_WIDTH - 1) * xf
    halo[HALO:2 * HALO, :] = xf[0:HALO]
    first = HALO - (CONV_WIDTH - 1)
    head = w(0) * halo[first:first + HALO, :]
    for j in range(1, CONV_WIDTH):
        head = head + w(j) * halo[first + j:first + j + HALO, :]
    halo[0:HALO, :] = xf[tm - HALO:tm]
    return jnp.concatenate([head, acc[HALO:]], axis=0)


def _delta_kernel(q_ref, k_ref, v_ref, z_ref, ab_ref, abt_ref, cw_ref, sh_ref, alr_ref, dtr_ref, alc_ref, dtc_ref, nw_ref,
                  o_ref, qbuf, kbuf, vbuf, q_s, k_s, v_s, o_s, gc_s, gr_s, bt_s, st_s):
    tm, HD = q_ref.shape
    C = DN_CHUNK
    DK = DN_DK
    n_chunks = tm // C

    @pl.when(pl.program_id(1) == 0)
    def _():
        for buf in (qbuf, kbuf, vbuf):
            buf[0:HALO, :] = jnp.zeros((HALO, HD), F32)
        st_s[...] = jnp.zeros_like(st_s)

    q_s[...] = _silu(_causal_conv_bf16(q_ref, qbuf, cw_ref, 0, sh_ref))
    k_s[...] = _silu(_causal_conv_bf16(k_ref, kbuf, cw_ref, HD, sh_ref))
    v_s[...] = _silu(_causal_conv_bf16(v_ref, vbuf, cw_ref, 2 * HD, sh_ref))
    for h in range(DN_HEADS):
        sl = slice(h * DK, (h + 1) * DK)
        qh = q_s[:, sl]
        q_s[:, sl] = qh * lax.rsqrt(jnp.sum(qh * qh, axis=-1, keepdims=True) + NORM_EPS) * (DK ** -0.5)
        kh = k_s[:, sl]
        k_s[:, sl] = kh * lax.rsqrt(jnp.sum(kh * kh, axis=-1, keepdims=True) + NORM_EPS)

    ab = ab_ref[...]
    bt_s[...] = _sigmoid(ab)
    g_rows = -jnp.exp(alr_ref[...]) * _softplus(ab + dtr_ref[...])
    g_cols = -jnp.exp(alc_ref[...]) * _softplus(abt_ref[...] + dtc_ref[...])
    ri = lax.broadcasted_iota(I32, (C, C), 0)
    ci = lax.broadcasted_iota(I32, (C, C), 1)
    causal = ri >= ci
    strict = ri > ci
    lower = causal.astype(F32)
    upper = (ri <= ci).astype(F32)
    for c in range(n_chunks):
        gc_s[c * C:(c + 1) * C, :] = jnp.dot(lower, g_rows[c * C:(c + 1) * C, :], preferred_element_type=F32, precision=HIGHEST)
        gr_s[c] = jnp.dot(g_cols[:, c * C:(c + 1) * C], upper, preferred_element_type=F32, precision=HIGHEST)
    eye = (ri == ci).astype(F32)
    diag_blk = strict & ((ri >> 3) == (ci >> 3))
    merge_masks = []
    sh = 3
    while (1 << sh) < C:
        merge_masks.append(((ri >> (sh + 1)) == (ci >> (sh + 1))) & (((ri >> sh) & 1) == 1) & (((ci >> sh) & 1) == 0))
        sh += 1

    units = [(c, h) for c in range(n_chunks) for h in range(DN_HEADS)]
    ge, kd, qd, rhs, qk, x = {}, {}, {}, {}, {}, {}
    for c in range(n_chunks):
        r0 = c * C
        g_all = gc_s[r0:r0 + C, :]
        g_last = gc_s[r0 + C - 1:r0 + C, :]
        eg_all = jnp.exp(g_all)
        ekd_all = jnp.exp(g_last - g_all)
        ge_all = jnp.exp(g_last)
        beta_all = bt_s[r0:r0 + C, :]
        gr_all = gr_s[c]
        for h in range(DN_HEADS):
            u_ = (c, h)
            sl = slice(h * DK, (h + 1) * DK)
            gl = DN_HEADS + h
            qh = q_s[r0:r0 + C, sl]
            kh = k_s[r0:r0 + C, sl]
            beta = beta_all[:, h:h + 1]
            eg = eg_all[:, gl:gl + 1]
            decay = jnp.exp(jnp.where(causal, g_all[:, gl:gl + 1] - gr_all[gl:gl + 1, :], -jnp.inf))
            kb = kh * beta
            both = _mm_nt(jnp.concatenate([kb, qh], axis=0), kh)
            x[u_] = jnp.where(strict, both[:C] * decay, 0.0)
            qk[u_] = jnp.where(causal, both[C:] * decay, 0.0).astype(BF16)
            rhs[u_] = jnp.concatenate([v_s[r0:r0 + C, sl] * beta, kb * eg], axis=1).astype(BF16)
            qd[u_] = (qh * eg).astype(BF16)
            kd[u_] = (kh * ekd_all[:, gl:gl + 1]).astype(BF16)
            ge[u_] = ge_all[:, gl:gl + 1]
    p = {u_: jnp.where(diag_blk, -x[u_], 0.0) for u_ in units}
    t = {u_: eye + p[u_] for u_ in units}
    for _ in range(2):
        p = {u_: _mm(p[u_], p[u_]) for u_ in units}
        t = {u_: t[u_] + _mm(t[u_], p[u_]) for u_ in units}
    for m in merge_masks:
        y = {u_: _mm(jnp.where(m, x[u_], 0.0), t[u_]) for u_ in units}
        t = {u_: t[u_] - _mm(t[u_], y[u_]) for u_ in units}
    sol = {u_: _mm(t[u_], rhs[u_]) for u_ in units}
    for c in range(n_chunks):
        r0 = c * C
        heads = [(c, h) for h in range(DN_HEADS)]
        st = {u_: st_s[u_[1]] for u_ in heads}
        ws = {u_: _mm(jnp.concatenate([sol[u_][:, DK:], qd[u_]], axis=0), st[u_]) for u_ in heads}
        v_new = {u_: (sol[u_][:, :DK] - ws[u_][:C]).astype(BF16) for u_ in heads}
        for u_ in heads:
            h = u_[1]
            o_s[r0:r0 + C, h * DK:(h + 1) * DK] = ws[u_][C:] + _mm(qk[u_], v_new[u_])
            st_s[h] = st[u_] * ge[u_] + _mm_tn(kd[u_], v_new[u_])

    for h in range(DN_HEADS):
        sl = slice(h * DK, (h + 1) * DK)
        oh = o_s[:, sl]
        y = oh * lax.rsqrt(jnp.mean(oh * oh, axis=-1, keepdims=True) + NORM_EPS) * nw_ref[...]
        o_ref[:, sl] = (y * _silu(z_ref[:, sl].astype(F32))).astype(BF16)


def _delta(proj, ab, abt, cw, alr, dtr, alc, dtc, nw, B, S, tm):
    T = proj.shape[1]
    HD = DN_HEADS * DN_DK
    spb = S // tm
    n_chunks = tm // DN_CHUNK
    col = lambda k: pl.BlockSpec((None, tm, HD), lambda b, j: (k, b * spb + j, 0))
    const = lambda shape: pl.BlockSpec(shape, lambda b, j: (0,) * len(shape))
    t_out = jnp.arange((CONV_WIDTH - 1) * tm, dtype=I32)[:, None]
    shifts = (jnp.arange(tm, dtype=I32)[None, :] == t_out % tm - (CONV_WIDTH - 1 - t_out // tm)).astype(BF16)
    return pl.pallas_call(
        _delta_kernel,
        grid=(B, spb),
        in_specs=[
            col(0), col(1), col(2), col(3),
            pl.BlockSpec((tm, LANES), lambda b, j: (b * spb + j, 0)),
            pl.BlockSpec((None, 2 * DN_HEADS, tm), lambda b, j: (b, 0, j)),
            const((CONV_WIDTH, 3 * HD)),
            const(((CONV_WIDTH - 1) * tm, tm)),
            const((1, LANES)), const((1, LANES)),
            const((2 * DN_HEADS, 1)), const((2 * DN_HEADS, 1)),
            const((1, DN_DK)),
        ],
        out_specs=pl.BlockSpec((tm, HD), lambda b, j: (b * spb + j, 0)),
        out_shape=jax.ShapeDtypeStruct((T, HD), BF16),
        scratch_shapes=[
            pltpu.VMEM((2 * HALO, HD), F32), pltpu.VMEM((2 * HALO, HD), F32), pltpu.VMEM((2 * HALO, HD), F32),
            pltpu.VMEM((tm, HD), F32), pltpu.VMEM((tm, HD), F32), pltpu.VMEM((tm, HD), F32), pltpu.VMEM((tm, HD), F32),
            pltpu.VMEM((tm, LANES), F32),
            pltpu.VMEM((n_chunks, 2 * DN_HEADS, DN_CHUNK), F32),
            pltpu.VMEM((tm, LANES), F32),
            pltpu.VMEM((DN_HEADS, DN_DK, DN_DK), F32),
        ],
        compiler_params=_params("arbitrary", "arbitrary"),
        name="delta",
    )(proj, proj, proj, proj, ab, abt, cw, shifts, alr, dtr, alc, dtc, nw)


def _pack_bf16_pairs(h):
    n = h.shape[1] // 2
    hb = h.astype(BF16).astype(F32)
    lo = lax.bitcast_convert_type(hb[:, :n], U32)
    hi = lax.bitcast_convert_type(hb[:, n:], U32)
    return hi | (lo >> 16)


def _unpack_bf16_pairs(p):
    lo = lax.bitcast_convert_type(p << 16, F32)
    hi = lax.bitcast_convert_type(p & jnp.uint32(0xFFFF0000), F32)
    return jnp.concatenate([lo, hi], axis=1).astype(BF16)


def _merge_kernel(yrg_ref, ydn_ref, grg_ref, gdn_ref, x_ref, mod_ref, wbr_ref, wbd_ref, wo_ref, nw_ref, wr_ref, bias_ref,
                  x1_ref, hp_ref, slab_ref, slabt_ref, cnt_ref, carry, lg_s):
    i = pl.program_id(0)

    @pl.when(i == 0)
    def _():
        carry[...] = jnp.zeros_like(carry)
        lg_s[...] = jnp.zeros_like(lg_s)

    slab = _route_tile(lg_s[...], carry, jnp.where(i > 0, 1.0, 0.0))
    slab_ref[...] = slab
    slabt_ref[...] = jnp.transpose(slab)[0:SUBLANES, :]
    cnt_ref[...] = carry[...]

    m = (_sigmoid(grg_ref[...].astype(F32)) * jnp.dot(yrg_ref[...], wbr_ref[...], preferred_element_type=F32)
         + _sigmoid(gdn_ref[...].astype(F32)) * jnp.dot(ydn_ref[...], wbd_ref[...], preferred_element_type=F32))
    x1 = x_ref[...] + mod_ref[2:3, :] * jnp.dot(m.astype(BF16), wo_ref[...], preferred_element_type=F32)
    x1_ref[...] = x1
    y = x1 * lax.rsqrt(jnp.mean(x1 * x1, axis=-1, keepdims=True) + NORM_EPS) * nw_ref[...]
    h2 = y * (1.0 + mod_ref[4:5, :]) + mod_ref[3:4, :]
    h_hi = h2.astype(BF16)
    h_lo = (h2 - h_hi.astype(F32)).astype(BF16)
    hh = jnp.dot(h_hi, wr_ref[...], preferred_element_type=F32)
    lg_s[...] = (hh[:, :LANES] + (hh[:, LANES:] + jnp.dot(h_lo, wr_ref[:, :LANES], preferred_element_type=F32))) + bias_ref[...]
    hp_ref[...] = _pack_bf16_pairs(h2)


def _merge(y_rg, y_dn, proj, x2, mod3, wbr, wbd, wo, nw, wr, bias, S, tm):
    T, D = x2.shape
    spb = S // tm
    nt = T // tm
    cur = lambda i: jnp.minimum(i, nt - 1)
    prev = lambda i: jnp.maximum(i - 1, 0)
    tile = lambda: pl.BlockSpec((tm, D), lambda i: (cur(i), 0))
    blk = lambda k: pl.BlockSpec((None, tm, D), lambda i: (k, cur(i), 0))
    const = lambda shape: pl.BlockSpec(shape, lambda i: (0,) * len(shape))
    return pl.pallas_call(
        _merge_kernel,
        grid=(nt + 1,),
        in_specs=[
            tile(), tile(), blk(4), blk(5), tile(),
            pl.BlockSpec((None, 6, D), lambda i: (cur(i) // spb, 0, 0)),
            const((D, D)), const((D, D)), const((D, D)), const((1, D)), const((D, 2 * LANES)), const((1, LANES)),
        ],
        out_specs=[tile(), pl.BlockSpec((tm, D // 2), lambda i: (cur(i), 0)),
                   pl.BlockSpec((tm, LANES), lambda i: (prev(i), 0)),
                   pl.BlockSpec((SUBLANES, tm), lambda i: (0, prev(i))), const((1, LANES))],
        out_shape=[jax.ShapeDtypeStruct((T, D), F32), jax.ShapeDtypeStruct((T, D // 2), U32),
                   jax.ShapeDtypeStruct((T, LANES), F32), jax.ShapeDtypeStruct((SUBLANES, T), F32),
                   jax.ShapeDtypeStruct((1, LANES), F32)],
        scratch_shapes=[pltpu.VMEM((1, LANES), F32), pltpu.VMEM((tm, LANES), F32)],
        compiler_params=_params("arbitrary"),
        name="merge",
    )(y_rg, y_dn, proj, proj, x2, mod3, wbr, wbd, wo, nw, wr, bias)


def _route_tile(lg, carry, live):
    tm = lg.shape[0]
    lane = lax.broadcasted_iota(I32, (tm, LANES), 1)
    big = jnp.int32(LANES)
    ninf = -jnp.inf

    def first_max(vals):
        vmax = jnp.max(vals, axis=-1, keepdims=True)
        return vmax, jnp.min(jnp.where(vals == vmax, lane, big), axis=-1, keepdims=True)

    is_g = lane < N_GROUPS
    gmax, gsel = first_max(jnp.where(is_g, lg, ninf))
    p_g = 1.0 / jnp.sum(jnp.where(is_g, jnp.exp(lg - gmax), 0.0), axis=-1, keepdims=True)
    e_lane = lane - N_GROUPS
    in_group = (e_lane >= 0) & (e_lane < N_EXPERTS) & ((e_lane >> 3) == gsel)
    el = jnp.where(in_group, lg, ninf)
    v1, l1 = first_max(el)
    v2, l2 = first_max(jnp.where(lane == l1, ninf, el))
    ex = jnp.exp(v2 - v1)
    w1 = p_g / (1.0 + ex)
    w2 = p_g * ex / (1.0 + ex)
    e1 = l1 - N_GROUPS
    e2 = l2 - N_GROUPS

    hit1 = lane == e1
    hit2 = lane == e2
    onehot = jnp.where(hit1 | hit2, 1.0, 0.0)
    ri = lax.broadcasted_iota(I32, (tm, tm), 0)
    ci = lax.broadcasted_iota(I32, (tm, tm), 1)
    before = jnp.where(ri > ci, 1.0, 0.0).astype(BF16)
    cum = jnp.dot(before, onehot.astype(BF16), preferred_element_type=F32) + carry[...]
    r1 = jnp.sum(jnp.where(hit1, cum, 0.0), axis=-1, keepdims=True)
    r2 = jnp.sum(jnp.where(hit2, cum, 0.0), axis=-1, keepdims=True)
    carry[...] = carry[...] + live * jnp.sum(onehot, axis=0, keepdims=True)

    slab = jnp.where(lane == 0, e1.astype(F32), 0.0)
    slab = jnp.where(lane == 1, e2.astype(F32), slab)
    slab = jnp.where(lane == 2, r1, slab)
    slab = jnp.where(lane == 3, r2, slab)
    slab = jnp.where(lane == 4, w1, slab)
    return jnp.where(lane == 5, w2, slab)


def _row_copy(src, dst, s, d, sem):
    return pltpu.make_async_copy(src.at[pl.ds(s, 1), :], dst.at[pl.ds(d, 1), :], sem)


def _sort_kernel(dest_ref, hp_ref, xs_in_ref, xs_ref, sem, *, tb):
    del xs_in_ref

    def issue(r, carry):
        _row_copy(hp_ref, xs_ref, r, dest_ref[0, r], sem).start()
        _row_copy(hp_ref, xs_ref, r, dest_ref[0, tb + r], sem).start()
        return carry

    lax.fori_loop(0, tb, issue, 0, unroll=8)

    def drain(r, carry):
        _row_copy(hp_ref, xs_ref, 0, 0, sem).wait()
        _row_copy(hp_ref, xs_ref, 0, 0, sem).wait()
        return carry

    lax.fori_loop(0, tb, drain, 0, unroll=8)


def _sort(dest3, hp, xs0, tb):
    T, Dh = hp.shape
    return pl.pallas_call(
        functools.partial(_sort_kernel, tb=tb),
        grid=(T // tb,),
        in_specs=[
            pl.BlockSpec((None, 1, 2 * tb), lambda i: (i, 0, 0), memory_space=pltpu.SMEM),
            pl.BlockSpec((tb, Dh), lambda i: (i, 0)),
            pl.BlockSpec(memory_space=pl.ANY),
        ],
        out_specs=pl.BlockSpec(memory_space=pl.ANY),
        out_shape=jax.ShapeDtypeStruct(xs0.shape, xs0.dtype),
        scratch_shapes=[pltpu.SemaphoreType.DMA],
        input_output_aliases={2: 0},
        compiler_params=pltpu.CompilerParams(dimension_semantics=("arbitrary",), has_side_effects=True),
        name="sort",
    )(dest3, hp, xs0)


def _experts_kernel(be_ref, nv_ref, x_ref, wg_ref, wu_ref, wd_ref, y_ref, wg_b, wu_b, wd_b):
    i = pl.program_id(0)
    valid = i < nv_ref[0]

    @pl.when(valid & ((i == 0) | (be_ref[i] != be_ref[jnp.maximum(i - 1, 0)])))
    def _():
        wg_b[...] = wg_ref[...].astype(BF16)
        wu_b[...] = wu_ref[...].astype(BF16)
        wd_b[...] = wd_ref[...].astype(BF16)

    @pl.when(valid)
    def _():
        x = _unpack_bf16_pairs(x_ref[...])
        g = jnp.dot(x, wg_b[...], preferred_element_type=F32)
        u = jnp.dot(x, wu_b[...], preferred_element_type=F32)
        y_ref[...] = jnp.dot((_silu(g) * u).astype(BF16), wd_b[...], preferred_element_type=F32)

    @pl.when(jnp.logical_not(valid))
    def _():
        y_ref[...] = jnp.zeros_like(y_ref)


def _experts(blk_e, nvb, xs, wg, wu, wd):
    P, Dh = xs.shape
    E, D, DE = wg.shape
    NB = P // MOE_BLOCK
    last = lambda i, nv: jnp.minimum(i, nv[0] - 1)
    grid_spec = pltpu.PrefetchScalarGridSpec(
        num_scalar_prefetch=2,
        grid=(NB,),
        in_specs=[
            pl.BlockSpec((MOE_BLOCK, Dh), lambda i, be, nv: (last(i, nv), 0)),
            pl.BlockSpec((None, D, DE), lambda i, be, nv: (be[last(i, nv)], 0, 0)),
            pl.BlockSpec((None, D, DE), lambda i, be, nv: (be[last(i, nv)], 0, 0)),
            pl.BlockSpec((None, DE, D), lambda i, be, nv: (be[last(i, nv)], 0, 0)),
        ],
        out_specs=pl.BlockSpec((MOE_BLOCK, D), lambda i, be, nv: (i, 0)),
        scratch_shapes=[pltpu.VMEM((D, DE), BF16), pltpu.VMEM((D, DE), BF16), pltpu.VMEM((DE, D), BF16)],
    )
    return pl.pallas_call(
        _experts_kernel,
        grid_spec=grid_spec,
        out_shape=jax.ShapeDtypeStruct((P, D), F32),
        compiler_params=_params("arbitrary"),
        name="experts",
    )(blk_e, nvb, xs, wg, wu, wd)


def _final_kernel(dest_ref, next_ref, x1_ref, slab_ref, mod_ref, fw_ref, ys_ref, o_ref, ybuf, sems, *, tb):
    i = pl.program_id(0)
    nt = pl.num_programs(0)

    def gather(idx_ref, slot):
        def issue(r, carry):
            _row_copy(ys_ref, ybuf.at[slot, 0], idx_ref[0, r], r, sems.at[slot]).start()
            _row_copy(ys_ref, ybuf.at[slot, 1], idx_ref[0, tb + r], r, sems.at[slot]).start()
            return carry

        lax.fori_loop(0, tb, issue, 0, unroll=8)

    @pl.when(i == 0)
    def _():
        gather(dest_ref, 0)

    slot = i % 2

    @pl.when(i + 1 < nt)
    def _():
        gather(next_ref, 1 - slot)

    def drain(r, carry):
        _row_copy(ys_ref, ybuf.at[slot, 0], 0, 0, sems.at[slot]).wait()
        _row_copy(ys_ref, ybuf.at[slot, 1], 0, 0, sems.at[slot]).wait()
        return carry

    lax.fori_loop(0, tb, drain, 0, unroll=8)

    slab = slab_ref[...]
    y = slab[:, 4:5] * ybuf[slot, 0] + slab[:, 5:6] * ybuf[slot, 1]
    x2 = x1_ref[...] + mod_ref[5:6, :] * y
    o_ref[...] = x2 * lax.rsqrt(jnp.mean(x2 * x2, axis=-1, keepdims=True) + NORM_EPS) * fw_ref[...]


def _final(dest3, x1, slab, mod3, fw, ys, S, tb):
    T, D = x1.shape
    spb = S // tb
    nt = T // tb
    return pl.pallas_call(
        functools.partial(_final_kernel, tb=tb),
        grid=(nt,),
        in_specs=[
            pl.BlockSpec((None, 1, 2 * tb), lambda i: (i, 0, 0), memory_space=pltpu.SMEM),
            pl.BlockSpec((None, 1, 2 * tb), lambda i: (jnp.minimum(i + 1, nt - 1), 0, 0), memory_space=pltpu.SMEM),
            pl.BlockSpec((tb, D), lambda i: (i, 0)),
            pl.BlockSpec((tb, LANES), lambda i: (i, 0)),
            pl.BlockSpec((None, 6, D), lambda i: (i // spb, 0, 0)),
            pl.BlockSpec((1, D), lambda i: (0, 0)),
            pl.BlockSpec(memory_space=pl.ANY),
        ],
        out_specs=pl.BlockSpec((tb, D), lambda i: (i, 0)),
        out_shape=jax.ShapeDtypeStruct((T, D), F32),
        scratch_shapes=[pltpu.VMEM((2, 2, tb, D), F32), pltpu.SemaphoreType.DMA((2,))],
        compiler_params=_params("arbitrary"),
        name="final",
    )(dest3, dest3, x1, slab, mod3, fw, ys)


def _lane_row(vals, offset):
    return jnp.zeros((1, LANES), F32).at[0, offset:offset + vals.shape[0]].set(vals.astype(F32))


def _forward(x, c, w_ada, b_ada, norm1_w, w_in, rg_conv_w, rg_conv_b, rg_gate_a_w, rg_gate_a_b, rg_gate_x_w, rg_gate_x_b,
             rg_lambda, dn_conv_w, dn_a_log, dn_dt_bias, dn_norm_w, w_branch_rg, w_branch_dn, w_out, norm2_w,
             moe_w_group, moe_b_group, moe_w_router, moe_b_router, moe_w_gate, moe_w_up, moe_w_down, final_norm_w,
             tm_proj=512, tm_delta=256, tm_tok=512):
    B, S, D = x.shape
    T = B * S
    H = DN_HEADS
    x2 = x.reshape(T, D)
    mod3 = _ada(c, w_ada[0], b_ada[0]).reshape(B, 6, D)

    w = w_in[0]
    n_wide = 6 * D
    w_main = jnp.concatenate([w[:, :n_wide], w[:, n_wide + 2 * H:]], axis=1).astype(BF16)
    w_blocks = w_main.reshape(D, -1, D).transpose(1, 0, 2)
    w_small = jnp.pad(w[:, n_wide:n_wide + 2 * H], ((0, 0), (0, LANES - 2 * H))).astype(BF16)
    proj, ab, y_rg = _inproj_rg(x2, mod3, norm1_w, w_blocks, w_small, rg_conv_w[0], rg_conv_b[0],
                                rg_gate_a_w[0].astype(BF16), rg_gate_a_b.reshape(-1),
                                rg_gate_x_w[0].astype(BF16), rg_gate_x_b.reshape(-1), rg_lambda[0], S, tm_proj)

    abt = ab[:, :2 * H].reshape(B, S, 2 * H).transpose(0, 2, 1)
    zeros_h = jnp.zeros((H,), F32)
    col = lambda v: jnp.concatenate([zeros_h, v.astype(F32)]).reshape(2 * H, 1)
    y_dn = _delta(proj, ab, abt, dn_conv_w[0], _lane_row(dn_a_log[0], H), _lane_row(dn_dt_bias[0], H),
                  col(dn_a_log[0]), col(dn_dt_bias[0]), dn_norm_w, B, S, tm_delta)

    w_route = jnp.pad(jnp.concatenate([moe_w_group[0], moe_w_router[0]], axis=1), ((0, 0), (0, LANES - N_GROUPS - N_EXPERTS)))
    w_route_hi = w_route.astype(BF16)
    w_route = jnp.concatenate([w_route_hi, (w_route - w_route_hi.astype(F32)).astype(BF16)], axis=1)
    bias = _lane_row(jnp.concatenate([moe_b_group[0], moe_b_router[0]]), 0)
    x1, hp, slab, slabt, cnt = _merge(y_rg, y_dn, proj, x2, mod3, w_branch_rg[0].astype(BF16),
                                      w_branch_dn[0].astype(BF16), w_out[0].astype(BF16), norm2_w, w_route, bias, S, tm_proj)

    eid = slabt[0:2].astype(I32)
    rank = slabt[2:4].astype(I32)
    counts = cnt[0, :N_EXPERTS].astype(I32)
    padded = (counts + MOE_BLOCK - 1) // MOE_BLOCK * MOE_BLOCK
    pad_end = jnp.cumsum(padded)
    pad_start = pad_end - padded
    seg = jnp.sum(jnp.where(eid[:, :, None] == jnp.arange(N_EXPERTS, dtype=I32), pad_start, 0), axis=-1)
    dest = seg + rank
    nt = T // tm_tok
    dest3 = dest.reshape(2, nt, tm_tok).transpose(1, 0, 2).reshape(nt, 1, 2 * tm_tok)
    A = 2 * T
    P = (A + N_EXPERTS * (MOE_BLOCK - 1) + MOE_BLOCK - 1) // MOE_BLOCK * MOE_BLOCK
    NB = P // MOE_BLOCK
    starts = jnp.arange(NB, dtype=I32) * MOE_BLOCK
    blk_e = jnp.minimum(jnp.sum(pad_end[None, :] <= starts[:, None], axis=1), N_EXPERTS - 1).astype(I32)
    nvb = (pad_end[-1:] // MOE_BLOCK).astype(I32)

    xs = _sort(dest3, hp, jnp.zeros((P, D // 2), U32), tm_tok)
    ys = _experts(blk_e, nvb, xs, moe_w_gate[0], moe_w_up[0], moe_w_down[0])
    out = _final(dest3, x1, slab, mod3, final_norm_w.reshape(1, D), ys, S, tm_tok)
    return out.reshape(B, S, D)


def kernel(x, c, w_ada, b_ada, norm1_w, w_in, rg_conv_w, rg_conv_b, rg_gate_a_w, rg_gate_a_b, rg_gate_x_w, rg_gate_x_b, rg_lambda, dn_conv_w, dn_a_log, dn_dt_bias, dn_norm_w, w_branch_rg, w_branch_dn, w_out, norm2_w, moe_w_group, moe_b_group, moe_w_router, moe_b_router, moe_w_gate, moe_w_up, moe_w_down, final_norm_w):
    return _forward(x, c, w_ada, b_ada, norm1_w, w_in, rg_conv_w, rg_conv_b, rg_gate_a_w, rg_gate_a_b, rg_gate_x_w, rg_gate_x_b, rg_lambda, dn_conv_w, dn_a_log, dn_dt_bias, dn_norm_w, w_branch_rg, w_branch_dn, w_out, norm2_w, moe_w_group, moe_b_group, moe_w_router, moe_b_router, moe_w_gate, moe_w_up, moe_w_down, final_norm_w)
```

```python
import functools

import jax
import jax.numpy as jnp
from jax import lax
from jax.experimental import pallas as pl
from jax.experimental.pallas import tpu as pltpu

F32 = jnp.float32
BF16 = jnp.bfloat16
U32 = jnp.uint32
I32 = jnp.int32
HIGHEST = lax.Precision.HIGHEST

NORM_EPS = 1e-6
RG_C = 8.0
RG_BLOCKS = 4
CONV_WIDTH = 4
DN_HEADS = 8
DN_DK = 128
DN_CHUNK = 64
N_GROUPS = 4
EXPERTS_PER_GROUP = 8
N_EXPERTS = N_GROUPS * EXPERTS_PER_GROUP
MOE_BLOCK = 512
LANES = 128
SUBLANES = 8
HALO = SUBLANES
VMEM_LIMIT = 48 * 1024 * 1024
VMEM_LIMIT_BIG = 56 * 1024 * 1024


def _sigmoid(x):
    return 1.0 / (1.0 + jnp.exp(-x))


def _silu(x):
    return x * _sigmoid(x)


def _softplus(x):
    return jnp.maximum(x, 0.0) + jnp.log(1.0 + jnp.exp(-jnp.abs(x)))


def _gelu_tanh(x):
    return x * (0.5 * (1.0 + jnp.tanh(0.7978845608028654 * (x + 0.044715 * (x * x * x)))))


def _mm(a, b):
    return jnp.dot(a.astype(BF16), b.astype(BF16), preferred_element_type=F32)


def _mm_nt(a, b):
    return lax.dot_general(a.astype(BF16), b.astype(BF16), (((1,), (1,)), ((), ())), preferred_element_type=F32)


def _mm_tn(a, b):
    return lax.dot_general(a.astype(BF16), b.astype(BF16), (((0,), (0,)), ((), ())), preferred_element_type=F32)


def _params(*sem, limit=None):
    return pltpu.CompilerParams(dimension_semantics=sem, vmem_limit_bytes=limit or VMEM_LIMIT)


def _ada_kernel(c_ref, w_ref, b_ref, o_ref):
    o_ref[...] = jnp.dot(_silu(c_ref[...]), w_ref[...], preferred_element_type=F32, precision=HIGHEST) + b_ref[...]


def _ada(c, w_ada, b_ada):
    B, D = c.shape
    N = w_ada.shape[1]
    rows = -(-B // SUBLANES) * SUBLANES
    cp = jnp.pad(c, ((0, rows - B), (0, 0)))
    tn = 1536
    out = pl.pallas_call(
        _ada_kernel,
        grid=(N // tn,),
        in_specs=[
            pl.BlockSpec((rows, D), lambda n: (0, 0)),
            pl.BlockSpec((D, tn), lambda n: (0, n)),
            pl.BlockSpec((1, tn), lambda n: (0, n)),
        ],
        out_specs=pl.BlockSpec((rows, tn), lambda n: (0, n)),
        out_shape=jax.ShapeDtypeStruct((rows, N), F32),
        compiler_params=_params("arbitrary"),
        name="ada",
    )(cp, w_ada, b_ada.reshape(1, N))
    return out[:B]


def _causal_conv(x, buf, w_ref, c0, width, tm):
    buf[HALO:HALO + tm, :] = x
    first = HALO - (CONV_WIDTH - 1)
    acc = w_ref[0:1, c0:c0 + width] * buf[first:first + tm, :]
    for j in range(1, CONV_WIDTH):
        acc = acc + w_ref[j:j + 1, c0:c0 + width] * buf[first + j:first + j + tm, :]
    buf[0:HALO, :] = buf[tm:tm + HALO, :]
    return acc


def _inproj_rg_kernel(x_ref, mod_ref, nw_ref, w_ref, ws_ref, cw_ref, cb_ref, wa_ref, ba_ref, wx_ref, bx_ref, lam_ref,
                      o_ref, os_ref, yrg_ref, xbuf, r_s, i_s, g_s, xc_s, xb_s, hb_s, h_s, *, spb):
    tm = x_ref.shape[0]
    W = cw_ref.shape[2]
    bd = W // RG_BLOCKS
    G = SUBLANES
    n_blocks = w_ref.shape[0]
    n_tail = n_blocks - 4

    @pl.when(pl.program_id(0) % spb == 0)
    def _():
        xbuf[0:HALO, :] = jnp.zeros((HALO, W), F32)
        h_s[...] = jnp.zeros_like(h_s)

    x = x_ref[...]
    y = x * lax.rsqrt(jnp.mean(x * x, axis=-1, keepdims=True) + NORM_EPS) * nw_ref[...]
    hb_s[...] = (y * (1.0 + mod_ref[1:2, :]) + mod_ref[0:1, :]).astype(BF16)

    def proj(n):
        return jnp.dot(hb_s[...], w_ref[n], preferred_element_type=F32)

    xbuf[HALO:HALO + tm, :] = proj(0)
    first = HALO - (CONV_WIDTH - 1)
    for r0 in range(0, tm, 2 * G):
        acc = cw_ref[0] * xbuf[first + r0:first + r0 + 2 * G, :]
        for j in range(1, CONV_WIDTH):
            acc = acc + cw_ref[j] * xbuf[first + j + r0:first + j + r0 + 2 * G, :]
        acc = acc + cb_ref[...]
        xc_s[r0:r0 + 2 * G, :] = acc
        xb_s[r0:r0 + 2 * G, :] = acc.astype(BF16)
    xbuf[0:HALO, :] = xbuf[tm:tm + HALO, :]
    o_ref[0] = proj(2).astype(BF16)
    o_ref[1] = proj(3).astype(BF16)
    for g in range(RG_BLOCKS):
        r_s[:, g * bd:(g + 1) * bd] = jnp.dot(xb_s[:, g * bd:(g + 1) * bd], wa_ref[g], preferred_element_type=F32)
        i_s[:, g * bd:(g + 1) * bd] = jnp.dot(xb_s[:, g * bd:(g + 1) * bd], wx_ref[g], preferred_element_type=F32)
    g_s[...] = _gelu_tanh(proj(1))

    log_a_scale = (-RG_C) * _softplus(-lam_ref[...])
    row = lax.broadcasted_iota(I32, (G, W), 0)
    rows_per_trip = tm // n_tail

    def trip(k, h):
        o_ref[2 + k] = proj(4 + k).astype(BF16)
        for g in range(rows_per_trip // G):
            sl = pl.ds(pl.multiple_of(k * rows_per_trip + g * G, G), G)
            a = jnp.exp(log_a_scale * _sigmoid(r_s[sl, :] + ba_ref[...]))
            b = jnp.sqrt(1.0 - a * a) * (_sigmoid(i_s[sl, :] + bx_ref[...]) * xc_s[sl, :])
            s = 1
            while s < G:
                keep = row >= s
                b = jnp.where(keep, a * pltpu.roll(b, s, 0) + b, b)
                a = jnp.where(keep, a * pltpu.roll(a, s, 0), a)
                s *= 2
            hg = a * h + b
            r_s[sl, :] = hg
            h = jnp.broadcast_to(hg[G - 1:G, :], (G, W))
        return h

    h_s[...] = lax.fori_loop(0, n_tail, trip, h_s[...])
    os_ref[...] = jnp.dot(hb_s[...], ws_ref[...], preferred_element_type=F32)
    yrg_ref[...] = (r_s[...] * g_s[...]).astype(BF16)


def _inproj_rg(x2, mod3, norm_w, w_blocks, w_small, cw, cb, wa, ba, wx, bx, lam, S, tm):
    T, D = x2.shape
    n_blocks, _, W = w_blocks.shape
    spb = S // tm
    bd = W // RG_BLOCKS
    rep = lambda v, n: jnp.broadcast_to(v[..., None, :], v.shape[:-1] + (n, W))
    cw, cb = rep(cw, 2 * SUBLANES), rep(cb, 2 * SUBLANES)
    ba, bx, lam = rep(ba, SUBLANES), rep(bx, SUBLANES), rep(lam, SUBLANES)
    const = lambda shape: pl.BlockSpec(shape, lambda i: (0,) * len(shape))
    return pl.pallas_call(
        functools.partial(_inproj_rg_kernel, spb=spb),
        grid=(T // tm,),
        in_specs=[
            pl.BlockSpec((tm, D), lambda i: (i, 0)),
            pl.BlockSpec((None, 6, D), lambda i: (i // spb, 0, 0)),
            const((1, D)),
            pl.BlockSpec((n_blocks, D, W), lambda i: (0, 0, 0), pipeline_mode=pl.Buffered(1)),
            const((D, LANES)),
            const((CONV_WIDTH, 2 * SUBLANES, W)), const((2 * SUBLANES, W)),
            const((RG_BLOCKS, bd, bd)), const((SUBLANES, W)),
            const((RG_BLOCKS, bd, bd)), const((SUBLANES, W)),
            const((SUBLANES, W)),
        ],
        out_specs=[
            pl.BlockSpec((n_blocks - 2, tm, W), lambda i: (0, i, 0)),
            pl.BlockSpec((tm, LANES), lambda i: (i, 0)),
            pl.BlockSpec((tm, W), lambda i: (i, 0)),
        ],
        out_shape=[jax.ShapeDtypeStruct((n_blocks - 2, T, W), BF16), jax.ShapeDtypeStruct((T, LANES), F32),
                   jax.ShapeDtypeStruct((T, W), BF16)],
        scratch_shapes=[
            pltpu.VMEM((tm + HALO, W), F32),
            pltpu.VMEM((tm, W), F32),
            pltpu.VMEM((tm, W), F32),
            pltpu.VMEM((tm, W), F32),
            pltpu.VMEM((tm, W), F32),
            pltpu.VMEM((tm, W), BF16),
            pltpu.VMEM((tm, D), BF16),
            pltpu.VMEM((SUBLANES, W), F32),
        ],
        compiler_params=_params("arbitrary", limit=VMEM_LIMIT_BIG),
        name="inproj_rg",
    )(x2, mod3, norm_w, w_blocks, w_small, cw, cb, wa, ba, wx, bx, lam)


def _causal_conv_bf16(x_ref, halo, w_ref, c0, shifts_ref):
    tm, width = x_ref.shape
    xb = x_ref[...]
    xf = xb.astype(F32)
    w = lambda j: w_ref[j:j + 1, c0:c0 + width]
    shifted = jnp.dot(shifts_ref[...], xb, preferred_element_type=F32)
    acc = w(0) * shifted[0:tm]
    for j in range(1, CONV_WIDTH - 1):
        acc = acc + w(j) * shifted[j * tm:(j + 1) * tm]
    acc = acc + w(CONV_WIDTH - 1) * xf
    halo[HALO:2 * HALO, :] = xf[0:HALO]
    first = HALO - (CONV_WIDTH - 1)
    head = w(0) * halo[first:first + HALO, :]
    for j in range(1, CONV_WIDTH):
        head = head + w(j) * halo[first + j:first + j + HALO, :]
    halo[0:HALO, :] = xf[tm - HALO:tm]
    return jnp.concatenate([head, acc[HALO:]], axis=0)


def _delta_kernel(q_ref, k_ref, v_ref, z_ref, ab_ref, abt_ref, cw_ref, sh_ref, alr_ref, dtr_ref, alc_ref, dtc_ref, nw_ref,
                  o_ref, qbuf, kbuf, vbuf, q_s, k_s, v_s, o_s, gc_s, gr_s, bt_s, st_s):
    tm, HD = q_ref.shape
    C = DN_CHUNK
    DK = DN_DK
    n_chunks = tm // C

    @pl.when(pl.program_id(1) == 0)
    def _():
        for buf in (qbuf, kbuf, vbuf):
            buf[0:HALO, :] = jnp.zeros((HALO, HD), F32)
        st_s[...] = jnp.zeros_like(st_s)

    q_s[...] = _silu(_causal_conv_bf16(q_ref, qbuf, cw_ref, 0, sh_ref))
    k_s[...] = _silu(_causal_conv_bf16(k_ref, kbuf, cw_ref, HD, sh_ref))
    v_s[...] = _silu(_causal_conv_bf16(v_ref, vbuf, cw_ref, 2 * HD, sh_ref))
    for h in range(DN_HEADS):
        sl = slice(h * DK, (h + 1) * DK)
        qh = q_s[:, sl]
        q_s[:, sl] = qh * lax.rsqrt(jnp.sum(qh * qh, axis=-1, keepdims=True) + NORM_EPS) * (DK ** -0.5)
        kh = k_s[:, sl]
        k_s[:, sl] = kh * lax.rsqrt(jnp.sum(kh * kh, axis=-1, keepdims=True) + NORM_EPS)

    ab = ab_ref[...]
    bt_s[...] = _sigmoid(ab)
    g_rows = -jnp.exp(alr_ref[...]) * _softplus(ab + dtr_ref[...])
    g_cols = -jnp.exp(alc_ref[...]) * _softplus(abt_ref[...] + dtc_ref[...])
    ri = lax.broadcasted_iota(I32, (C, C), 0)
    ci = lax.broadcasted_iota(I32, (C, C), 1)
    causal = ri >= ci
    strict = ri > ci
    lower = causal.astype(F32)
    upper = (ri <= ci).astype(F32)
    for c in range(n_chunks):
        gc_s[c * C:(c + 1) * C, :] = jnp.dot(lower, g_rows[c * C:(c + 1) * C, :], preferred_element_type=F32, precision=HIGHEST)
        gr_s[c] = jnp.dot(g_cols[:, c * C:(c + 1) * C], upper, preferred_element_type=F32, precision=HIGHEST)
    eye = (ri == ci).astype(F32)
    diag_blk = strict & ((ri >> 3) == (ci >> 3))
    merge_masks = []
    sh = 3
    while (1 << sh) < C:
        merge_masks.append(((ri >> (sh + 1)) == (ci >> (sh + 1))) & (((ri >> sh) & 1) == 1) & (((ci >> sh) & 1) == 0))
        sh += 1

    units = [(c, h) for c in range(n_chunks) for h in range(DN_HEADS)]
    ge, kd, qd, rhs, qk, x = {}, {}, {}, {}, {}, {}
    for c in range(n_chunks):
        r0 = c * C
        g_all = gc_s[r0:r0 + C, :]
        g_last = gc_s[r0 + C - 1:r0 + C, :]
        eg_all = jnp.exp(g_all)
        ekd_all = jnp.exp(g_last - g_all)
        ge_all = jnp.exp(g_last)
        beta_all = bt_s[r0:r0 + C, :]
        gr_all = gr_s[c]
        for h in range(DN_HEADS):
            u_ = (c, h)
            sl = slice(h * DK, (h + 1) * DK)
            gl = DN_HEADS + h
            qh = q_s[r0:r0 + C, sl]
            kh = k_s[r0:r0 + C, sl]
            beta = beta_all[:, h:h + 1]
            eg = eg_all[:, gl:gl + 1]
            decay = jnp.exp(jnp.where(causal, g_all[:, gl:gl + 1] - gr_all[gl:gl + 1, :], -jnp.inf))
            kb = kh * beta
            both = _mm_nt(jnp.concatenate([kb, qh], axis=0), kh)
            x[u_] = jnp.where(strict, both[:C] * decay, 0.0)
            qk[u_] = jnp.where(causal, both[C:] * decay, 0.0).astype(BF16)
            rhs[u_] = jnp.concatenate([v_s[r0:r0 + C, sl] * beta, kb * eg], axis=1).astype(BF16)
            qd[u_] = (qh * eg).astype(BF16)
            kd[u_] = (kh * ekd_all[:, gl:gl + 1]).astype(BF16)
            ge[u_] = ge_all[:, gl:gl + 1]
    p = {u_: jnp.where(diag_blk, -x[u_], 0.0) for u_ in units}
    t = {u_: eye + p[u_] for u_ in units}
    for _ in range(2):
        p = {u_: _mm(p[u_], p[u_]) for u_ in units}
        t = {u_: t[u_] + _mm(t[u_], p[u_]) for u_ in units}
    for m in merge_masks:
        y = {u_: _mm(jnp.where(m, x[u_], 0.0), t[u_]) for u_ in units}
        t = {u_: t[u_] - _mm(t[u_], y[u_]) for u_ in units}
    sol = {u_: _mm(t[u_], rhs[u_]) for u_ in units}
    for c in range(n_chunks):
        r0 = c * C
        heads = [(c, h) for h in range(DN_HEADS)]
        st = {u_: st_s[u_[1]] for u_ in heads}
        ws = {u_: _mm(jnp.concatenate([sol[u_][:, DK:], qd[u_]], axis=0), st[u_]) for u_ in heads}
        v_new = {u_: (sol[u_][:, :DK] - ws[u_][:C]).astype(BF16) for u_ in heads}
        for u_ in heads:
            h = u_[1]
            o_s[r0:r0 + C, h * DK:(h + 1) * DK] = ws[u_][C:] + _mm(qk[u_], v_new[u_])
            st_s[h] = st[u_] * ge[u_] + _mm_tn(kd[u_], v_new[u_])

    for h in range(DN_HEADS):
        sl = slice(h * DK, (h + 1) * DK)
        oh = o_s[:, sl]
        y = oh * lax.rsqrt(jnp.mean(oh * oh, axis=-1, keepdims=True) + NORM_EPS) * nw_ref[...]
        o_ref[:, sl] = (y * _silu(z_ref[:, sl].astype(F32))).astype(BF16)


def _delta(proj, ab, abt, cw, alr, dtr, alc, dtc, nw, B, S, tm):
    T = proj.shape[1]
    HD = DN_HEADS * DN_DK
    spb = S // tm
    n_chunks = tm // DN_CHUNK
    col = lambda k: pl.BlockSpec((None, tm, HD), lambda b, j: (k, b * spb + j, 0))
    const = lambda shape: pl.BlockSpec(shape, lambda b, j: (0,) * len(shape))
    t_out = jnp.arange((CONV_WIDTH - 1) * tm, dtype=I32)[:, None]
    shifts = (jnp.arange(tm, dtype=I32)[None, :] == t_out % tm - (CONV_WIDTH - 1 - t_out // tm)).astype(BF16)
    return pl.pallas_call(
        _delta_kernel,
        grid=(B, spb),
        in_specs=[
            col(0), col(1), col(2), col(3),
            pl.BlockSpec((tm, LANES), lambda b, j: (b * spb + j, 0)),
            pl.BlockSpec((None, 2 * DN_HEADS, tm), lambda b, j: (b, 0, j)),
            const((CONV_WIDTH, 3 * HD)),
            const(((CONV_WIDTH - 1) * tm, tm)),
            const((1, LANES)), const((1, LANES)),
            const((2 * DN_HEADS, 1)), const((2 * DN_HEADS, 1)),
            const((1, DN_DK)),
        ],
        out_specs=pl.BlockSpec((tm, HD), lambda b, j: (b * spb + j, 0)),
        out_shape=jax.ShapeDtypeStruct((T, HD), BF16),
        scratch_shapes=[
            pltpu.VMEM((2 * HALO, HD), F32), pltpu.VMEM((2 * HALO, HD), F32), pltpu.VMEM((2 * HALO, HD), F32),
            pltpu.VMEM((tm, HD), F32), pltpu.VMEM((tm, HD), F32), pltpu.VMEM((tm, HD), F32), pltpu.VMEM((tm, HD), F32),
            pltpu.VMEM((tm, LANES), F32),
            pltpu.VMEM((n_chunks, 2 * DN_HEADS, DN_CHUNK), F32),
            pltpu.VMEM((tm, LANES), F32),
            pltpu.VMEM((DN_HEADS, DN_DK, DN_DK), F32),
        ],
        compiler_params=_params("arbitrary", "arbitrary"),
        name="delta",
    )(proj, proj, proj, proj, ab, abt, cw, shifts, alr, dtr, alc, dtc, nw)


def _pack_bf16_pairs(h):
    n = h.shape[1] // 2
    hb = h.astype(BF16).astype(F32)
    lo = lax.bitcast_convert_type(hb[:, :n], U32)
    hi = lax.bitcast_convert_type(hb[:, n:], U32)
    return hi | (lo >> 16)


def _unpack_bf16_pairs(p):
    lo = lax.bitcast_convert_type(p << 16, F32)
    hi = lax.bitcast_convert_type(p & jnp.uint32(0xFFFF0000), F32)
    return jnp.concatenate([lo, hi], axis=1).astype(BF16)


def _merge_kernel(yrg_ref, ydn_ref, grg_ref, gdn_ref, x_ref, mod_ref, wbr_ref, wbd_ref, wo_ref, nw_ref, wr_ref, bias_ref,
                  x1_ref, hp_ref, slab_ref, slabt_ref, cnt_ref, carry, lg_s):
    i = pl.program_id(0)

    @pl.when(i == 0)
    def _():
        carry[...] = jnp.zeros_like(carry)
        lg_s[...] = jnp.zeros_like(lg_s)

    slab = _route_tile(lg_s[...], carry, jnp.where(i > 0, 1.0, 0.0))
    slab_ref[...] = slab
    slabt_ref[...] = jnp.transpose(slab)[0:SUBLANES, :]
    cnt_ref[...] = carry[...]

    m = (_sigmoid(grg_ref[...].astype(F32)) * jnp.dot(yrg_ref[...], wbr_ref[...], preferred_element_type=F32)
         + _sigmoid(gdn_ref[...].astype(F32)) * jnp.dot(ydn_ref[...], wbd_ref[...], preferred_element_type=F32))
    x1 = x_ref[...] + mod_ref[2:3, :] * jnp.dot(m.astype(BF16), wo_ref[...], preferred_element_type=F32)
    x1_ref[...] = x1
    y = x1 * lax.rsqrt(jnp.mean(x1 * x1, axis=-1, keepdims=True) + NORM_EPS) * nw_ref[...]
    h2 = y * (1.0 + mod_ref[4:5, :]) + mod_ref[3:4, :]
    h_hi = h2.astype(BF16)
    h_lo = (h2 - h_hi.astype(F32)).astype(BF16)
    hh = jnp.dot(h_hi, wr_ref[...], preferred_element_type=F32)
    lg_s[...] = (hh[:, :LANES] + (hh[:, LANES:] + jnp.dot(h_lo, wr_ref[:, :LANES], preferred_element_type=F32))) + bias_ref[...]
    hp_ref[...] = _pack_bf16_pairs(h2)


def _merge(y_rg, y_dn, proj, x2, mod3, wbr, wbd, wo, nw, wr, bias, S, tm):
    T, D = x2.shape
    spb = S // tm
    nt = T // tm
    cur = lambda i: jnp.minimum(i, nt - 1)
    prev = lambda i: jnp.maximum(i - 1, 0)
    tile = lambda: pl.BlockSpec((tm, D), lambda i: (cur(i), 0))
    blk = lambda k: pl.BlockSpec((None, tm, D), lambda i: (k, cur(i), 0))
    const = lambda shape: pl.BlockSpec(shape, lambda i: (0,) * len(shape))
    return pl.pallas_call(
        _merge_kernel,
        grid=(nt + 1,),
        in_specs=[
            tile(), tile(), blk(4), blk(5), tile(),
            pl.BlockSpec((None, 6, D), lambda i: (cur(i) // spb, 0, 0)),
            const((D, D)), const((D, D)), const((D, D)), const((1, D)), const((D, 2 * LANES)), const((1, LANES)),
        ],
        out_specs=[tile(), pl.BlockSpec((tm, D // 2), lambda i: (cur(i), 0)),
                   pl.BlockSpec((tm, LANES), lambda i: (prev(i), 0)),
                   pl.BlockSpec((SUBLANES, tm), lambda i: (0, prev(i))), const((1, LANES))],
        out_shape=[jax.ShapeDtypeStruct((T, D), F32), jax.ShapeDtypeStruct((T, D // 2), U32),
                   jax.ShapeDtypeStruct((T, LANES), F32), jax.ShapeDtypeStruct((SUBLANES, T), F32),
                   jax.ShapeDtypeStruct((1, LANES), F32)],
        scratch_shapes=[pltpu.VMEM((1, LANES), F32), pltpu.VMEM((tm, LANES), F32)],
        compiler_params=_params("arbitrary"),
        name="merge",
    )(y_rg, y_dn, proj, proj, x2, mod3, wbr, wbd, wo, nw, wr, bias)


def _route_tile(lg, carry, live):
    tm = lg.shape[0]
    lane = lax.broadcasted_iota(I32, (tm, LANES), 1)
    big = jnp.int32(LANES)
    ninf = -jnp.inf

    def first_max(vals):
        vmax = jnp.max(vals, axis=-1, keepdims=True)
        return vmax, jnp.min(jnp.where(vals == vmax, lane, big), axis=-1, keepdims=True)

    is_g = lane < N_GROUPS
    gmax, gsel = first_max(jnp.where(is_g, lg, ninf))
    p_g = 1.0 / jnp.sum(jnp.where(is_g, jnp.exp(lg - gmax), 0.0), axis=-1, keepdims=True)
    e_lane = lane - N_GROUPS
    in_group = (e_lane >= 0) & (e_lane < N_EXPERTS) & ((e_lane >> 3) == gsel)
    el = jnp.where(in_group, lg, ninf)
    v1, l1 = first_max(el)
    v2, l2 = first_max(jnp.where(lane == l1, ninf, el))
    ex = jnp.exp(v2 - v1)
    w1 = p_g / (1.0 + ex)
    w2 = p_g * ex / (1.0 + ex)
    e1 = l1 - N_GROUPS
    e2 = l2 - N_GROUPS

    hit1 = lane == e1
    hit2 = lane == e2
    onehot = jnp.where(hit1 | hit2, 1.0, 0.0)
    ri = lax.broadcasted_iota(I32, (tm, tm), 0)
    ci = lax.broadcasted_iota(I32, (tm, tm), 1)
    before = jnp.where(ri > ci, 1.0, 0.0).astype(BF16)
    cum = jnp.dot(before, onehot.astype(BF16), preferred_element_type=F32) + carry[...]
    r1 = jnp.sum(jnp.where(hit1, cum, 0.0), axis=-1, keepdims=True)
    r2 = jnp.sum(jnp.where(hit2, cum, 0.0), axis=-1, keepdims=True)
    carry[...] = carry[...] + live * jnp.sum(onehot, axis=0, keepdims=True)

    slab = jnp.where(lane == 0, e1.astype(F32), 0.0)
    slab = jnp.where(lane == 1, e2.astype(F32), slab)
    slab = jnp.where(lane == 2, r1, slab)
    slab = jnp.where(lane == 3, r2, slab)
    slab = jnp.where(lane == 4, w1, slab)
    return jnp.where(lane == 5, w2, slab)


def _row_copy(src, dst, s, d, sem):
    return pltpu.make_async_copy(src.at[pl.ds(s, 1), :], dst.at[pl.ds(d, 1), :], sem)


def _sort_kernel(dest_ref, hp_ref, xs_in_ref, xs_ref, sem, *, tb):
    del xs_in_ref

    for r in range(tb):
        _row_copy(hp_ref, xs_ref, r, dest_ref[0, r], sem).start(priority=0)
        _row_copy(hp_ref, xs_ref, r, dest_ref[0, tb + r], sem).start(priority=1)

    def drain(r, carry):
        _row_copy(hp_ref, xs_ref, 0, 0, sem).wait()
        _row_copy(hp_ref, xs_ref, 0, 0, sem).wait()
        return carry

    lax.fori_loop(0, tb, drain, 0, unroll=8)


def _sort(dest3, hp, xs0, tb):
    T, Dh = hp.shape
    return pl.pallas_call(
        functools.partial(_sort_kernel, tb=tb),
        grid=(T // tb,),
        in_specs=[
            pl.BlockSpec((None, 1, 2 * tb), lambda i: (i, 0, 0), memory_space=pltpu.SMEM),
            pl.BlockSpec((tb, Dh), lambda i: (i, 0)),
            pl.BlockSpec(memory_space=pl.ANY),
        ],
        out_specs=pl.BlockSpec(memory_space=pl.ANY),
        out_shape=jax.ShapeDtypeStruct(xs0.shape, xs0.dtype),
        scratch_shapes=[pltpu.SemaphoreType.DMA],
        input_output_aliases={2: 0},
        compiler_params=pltpu.CompilerParams(dimension_semantics=("arbitrary",), has_side_effects=True),
        name="sort",
    )(dest3, hp, xs0)


def _experts_kernel(be_ref, nv_ref, x_ref, wg_ref, wu_ref, wd_ref, y_ref, wg_b, wu_b, wd_b):
    i = pl.program_id(0)
    valid = i < nv_ref[0]

    @pl.when(valid & ((i == 0) | (be_ref[i] != be_ref[jnp.maximum(i - 1, 0)])))
    def _():
        wg_b[...] = wg_ref[...].astype(BF16)
        wu_b[...] = wu_ref[...].astype(BF16)
        wd_b[...] = wd_ref[...].astype(BF16)

    @pl.when(valid)
    def _():
        x = _unpack_bf16_pairs(x_ref[...])
        g = jnp.dot(x, wg_b[...], preferred_element_type=F32)
        u = jnp.dot(x, wu_b[...], preferred_element_type=F32)
        y_ref[...] = jnp.dot((_silu(g) * u).astype(BF16), wd_b[...], preferred_element_type=F32)

    @pl.when(jnp.logical_not(valid))
    def _():
        y_ref[...] = jnp.zeros_like(y_ref)


def _experts(blk_e, nvb, xs, wg, wu, wd):
    P, Dh = xs.shape
    E, D, DE = wg.shape
    NB = P // MOE_BLOCK
    last = lambda i, nv: jnp.minimum(i, nv[0] - 1)
    grid_spec = pltpu.PrefetchScalarGridSpec(
        num_scalar_prefetch=2,
        grid=(NB,),
        in_specs=[
            pl.BlockSpec((MOE_BLOCK, Dh), lambda i, be, nv: (last(i, nv), 0)),
            pl.BlockSpec((None, D, DE), lambda i, be, nv: (be[last(i, nv)], 0, 0)),
            pl.BlockSpec((None, D, DE), lambda i, be, nv: (be[last(i, nv)], 0, 0)),
            pl.BlockSpec((None, DE, D), lambda i, be, nv: (be[last(i, nv)], 0, 0)),
        ],
        out_specs=pl.BlockSpec((MOE_BLOCK, D), lambda i, be, nv: (i, 0)),
        scratch_shapes=[pltpu.VMEM((D, DE), BF16), pltpu.VMEM((D, DE), BF16), pltpu.VMEM((DE, D), BF16)],
    )
    return pl.pallas_call(
        _experts_kernel,
        grid_spec=grid_spec,
        out_shape=jax.ShapeDtypeStruct((P, D), F32),
        compiler_params=_params("arbitrary"),
        name="experts",
    )(blk_e, nvb, xs, wg, wu, wd)


def _final_kernel(dest_ref, next_ref, x1_ref, slab_ref, mod_ref, fw_ref, ys_ref, o_ref, ybuf, sems, *, tb):
    i = pl.program_id(0)
    nt = pl.num_programs(0)

    def gather(idx_ref, slot):
        for r in range(tb):
            _row_copy(ys_ref, ybuf.at[slot, 0], idx_ref[0, r], r, sems.at[slot]).start(priority=0)
            _row_copy(ys_ref, ybuf.at[slot, 1], idx_ref[0, tb + r], r, sems.at[slot]).start(priority=1)

    def step(slot):
        @pl.when(i + 1 < nt)
        def _():
            gather(next_ref, 1 - slot)

        def drain(r, carry):
            _row_copy(ys_ref, ybuf.at[slot, 0], 0, 0, sems.at[slot]).wait()
            _row_copy(ys_ref, ybuf.at[slot, 1], 0, 0, sems.at[slot]).wait()
            return carry

        lax.fori_loop(0, tb, drain, 0, unroll=8)

        slab = slab_ref[...]
        y = slab[:, 4:5] * ybuf[slot, 0] + slab[:, 5:6] * ybuf[slot, 1]
        x2 = x1_ref[...] + mod_ref[5:6, :] * y
        o_ref[...] = x2 * lax.rsqrt(jnp.mean(x2 * x2, axis=-1, keepdims=True) + NORM_EPS) * fw_ref[...]

    @pl.when(i == 0)
    def _():
        gather(dest_ref, 0)

    for slot in range(2):
        pl.when(i % 2 == slot)(functools.partial(step, slot))


def _final(dest3, x1, slab, mod3, fw, ys, S, tb):
    T, D = x1.shape
    spb = S // tb
    nt = T // tb
    return pl.pallas_call(
        functools.partial(_final_kernel, tb=tb),
        grid=(nt,),
        in_specs=[
            pl.BlockSpec((None, 1, 2 * tb), lambda i: (i, 0, 0), memory_space=pltpu.SMEM),
            pl.BlockSpec((None, 1, 2 * tb), lambda i: (jnp.minimum(i + 1, nt - 1), 0, 0), memory_space=pltpu.SMEM),
            pl.BlockSpec((tb, D), lambda i: (i, 0)),
            pl.BlockSpec((tb, LANES), lambda i: (i, 0)),
            pl.BlockSpec((None, 6, D), lambda i: (i // spb, 0, 0)),
            pl.BlockSpec((1, D), lambda i: (0, 0)),
            pl.BlockSpec(memory_space=pl.ANY),
        ],
        out_specs=pl.BlockSpec((tb, D), lambda i: (i, 0)),
        out_shape=jax.ShapeDtypeStruct((T, D), F32),
        scratch_shapes=[pltpu.VMEM((2, 2, tb, D), F32), pltpu.SemaphoreType.DMA((2,))],
        compiler_params=_params("arbitrary"),
        name="final",
    )(dest3, dest3, x1, slab, mod3, fw, ys)


def _lane_row(vals, offset):
    return jnp.zeros((1, LANES), F32).at[0, offset:offset + vals.shape[0]].set(vals.astype(F32))


def _forward(x, c, w_ada, b_ada, norm1_w, w_in, rg_conv_w, rg_conv_b, rg_gate_a_w, rg_gate_a_b, rg_gate_x_w, rg_gate_x_b,
             rg_lambda, dn_conv_w, dn_a_log, dn_dt_bias, dn_norm_w, w_branch_rg, w_branch_dn, w_out, norm2_w,
             moe_w_group, moe_b_group, moe_w_router, moe_b_router, moe_w_gate, moe_w_up, moe_w_down, final_norm_w,
             tm_proj=512, tm_delta=256, tm_tok=512):
    B, S, D = x.shape
    T = B * S
    H = DN_HEADS
    x2 = x.reshape(T, D)
    mod3 = _ada(c, w_ada[0], b_ada[0]).reshape(B, 6, D)

    w = w_in[0]
    n_wide = 6 * D
    w_main = jnp.concatenate([w[:, :n_wide], w[:, n_wide + 2 * H:]], axis=1).astype(BF16)
    w_blocks = w_main.reshape(D, -1, D).transpose(1, 0, 2)
    w_small = jnp.pad(w[:, n_wide:n_wide + 2 * H], ((0, 0), (0, LANES - 2 * H))).astype(BF16)
    proj, ab, y_rg = _inproj_rg(x2, mod3, norm1_w, w_blocks, w_small, rg_conv_w[0], rg_conv_b[0],
                                rg_gate_a_w[0].astype(BF16), rg_gate_a_b.reshape(-1),
                                rg_gate_x_w[0].astype(BF16), rg_gate_x_b.reshape(-1), rg_lambda[0], S, tm_proj)

    abt = ab[:, :2 * H].reshape(B, S, 2 * H).transpose(0, 2, 1)
    zeros_h = jnp.zeros((H,), F32)
    col = lambda v: jnp.concatenate([zeros_h, v.astype(F32)]).reshape(2 * H, 1)
    y_dn = _delta(proj, ab, abt, dn_conv_w[0], _lane_row(dn_a_log[0], H), _lane_row(dn_dt_bias[0], H),
                  col(dn_a_log[0]), col(dn_dt_bias[0]), dn_norm_w, B, S, tm_delta)

    w_route = jnp.pad(jnp.concatenate([moe_w_group[0], moe_w_router[0]], axis=1), ((0, 0), (0, LANES - N_GROUPS - N_EXPERTS)))
    w_route_hi = w_route.astype(BF16)
    w_route = jnp.concatenate([w_route_hi, (w_route - w_route_hi.astype(F32)).astype(BF16)], axis=1)
    bias = _lane_row(jnp.concatenate([moe_b_group[0], moe_b_router[0]]), 0)
    x1, hp, slab, slabt, cnt = _merge(y_rg, y_dn, proj, x2, mod3, w_branch_rg[0].astype(BF16),
                                      w_branch_dn[0].astype(BF16), w_out[0].astype(BF16), norm2_w, w_route, bias, S, tm_proj)

    eid = slabt[0:2].astype(I32)
    rank = slabt[2:4].astype(I32)
    counts = cnt[0, :N_EXPERTS].astype(I32)
    padded = (counts + MOE_BLOCK - 1) // MOE_BLOCK * MOE_BLOCK
    pad_end = jnp.cumsum(padded)
    pad_start = pad_end - padded
    seg = jnp.sum(jnp.where(eid[:, :, None] == jnp.arange(N_EXPERTS, dtype=I32), pad_start, 0), axis=-1)
    dest = seg + rank
    nt = T // tm_tok
    dest3 = dest.reshape(2, nt, tm_tok).transpose(1, 0, 2).reshape(nt, 1, 2 * tm_tok)
    A = 2 * T
    P = (A + N_EXPERTS * (MOE_BLOCK - 1) + MOE_BLOCK - 1) // MOE_BLOCK * MOE_BLOCK
    NB = P // MOE_BLOCK
    starts = jnp.arange(NB, dtype=I32) * MOE_BLOCK
    blk_e = jnp.minimum(jnp.sum(pad_end[None, :] <= starts[:, None], axis=1), N_EXPERTS - 1).astype(I32)
    nvb = (pad_end[-1:] // MOE_BLOCK).astype(I32)

    xs = _sort(dest3, hp, jnp.zeros((P, D // 2), U32), tm_tok)
    ys = _experts(blk_e, nvb, xs, moe_w_gate[0], moe_w_up[0], moe_w_down[0])
    out = _final(dest3, x1, slab, mod3, final_norm_w.reshape(1, D), ys, S, tm_tok)
    return out.reshape(B, S, D)


def kernel(x, c, w_ada, b_ada, norm1_w, w_in, rg_conv_w, rg_conv_b, rg_gate_a_w, rg_gate_a_b, rg_gate_x_w, rg_gate_x_b, rg_lambda, dn_conv_w, dn_a_log, dn_dt_bias, dn_norm_w, w_branch_rg, w_branch_dn, w_out, norm2_w, moe_w_group, moe_b_group, moe_w_router, moe_b_router, moe_w_gate, moe_w_up, moe_w_down, final_norm_w):
    return _forward(x, c, w_ada, b_ada, norm1_w, w_in, rg_conv_w, rg_conv_b, rg_gate_a_w, rg_gate_a_b, rg_gate_x_w, rg_gate_x_b, rg_lambda, dn_conv_w, dn_a_log, dn_dt_bias, dn_norm_w, w_branch_rg, w_branch_dn, w_out, norm2_w, moe_w_group, moe_b_group, moe_w_router, moe_b_router, moe_w_gate, moe_w_up, moe_w_down, final_norm_w)
```

```python
import functools

import jax
import jax.numpy as jnp
from jax import lax
from jax.experimental import pallas as pl
from jax.experimental.pallas import tpu as pltpu

F32 = jnp.float32
BF16 = jnp.bfloat16
U32 = jnp.uint32
I32 = jnp.int32
HIGHEST = lax.Precision.HIGHEST

NORM_EPS = 1e-6
RG_C = 8.0
RG_BLOCKS = 4
CONV_WIDTH = 4
DN_HEADS = 8
DN_DK = 128
DN_CHUNK = 64
N_GROUPS = 4
EXPERTS_PER_GROUP = 8
N_EXPERTS = N_GROUPS * EXPERTS_PER_GROUP
MOE_BLOCK = 512
LANES = 128
SUBLANES = 8
HALO = SUBLANES
VMEM_LIMIT = 48 * 1024 * 1024
VMEM_LIMIT_BIG = 56 * 1024 * 1024


def _sigmoid(x):
    return 1.0 / (1.0 + jnp.exp(-x))


def _silu(x):
    return x * _sigmoid(x)


def _softplus(x):
    return jnp.maximum(x, 0.0) + jnp.log(1.0 + jnp.exp(-jnp.abs(x)))


def _gelu_tanh(x):
    return x * (0.5 * (1.0 + jnp.tanh(0.7978845608028654 * (x + 0.044715 * (x * x * x)))))


def _mm(a, b):
    return jnp.dot(a.astype(BF16), b.astype(BF16), preferred_element_type=F32)


def _mm_nt(a, b):
    return lax.dot_general(a.astype(BF16), b.astype(BF16), (((1,), (1,)), ((), ())), preferred_element_type=F32)


def _mm_tn(a, b):
    return lax.dot_general(a.astype(BF16), b.astype(BF16), (((0,), (0,)), ((), ())), preferred_element_type=F32)


def _params(*sem, limit=None):
    return pltpu.CompilerParams(dimension_semantics=sem, vmem_limit_bytes=limit or VMEM_LIMIT)


def _ada_kernel(c_ref, w_ref, b_ref, o_ref):
    o_ref[...] = jnp.dot(_silu(c_ref[...]), w_ref[...], preferred_element_type=F32, precision=HIGHEST) + b_ref[...]


def _ada(c, w_ada, b_ada):
    B, D = c.shape
    N = w_ada.shape[1]
    rows = -(-B // SUBLANES) * SUBLANES
    cp = jnp.pad(c, ((0, rows - B), (0, 0)))
    tn = 1536
    out = pl.pallas_call(
        _ada_kernel,
        grid=(N // tn,),
        in_specs=[
            pl.BlockSpec((rows, D), lambda n: (0, 0)),
            pl.BlockSpec((D, tn), lambda n: (0, n)),
            pl.BlockSpec((1, tn), lambda n: (0, n)),
        ],
        out_specs=pl.BlockSpec((rows, tn), lambda n: (0, n)),
        out_shape=jax.ShapeDtypeStruct((rows, N), F32),
        compiler_params=_params("arbitrary"),
        name="ada",
    )(cp, w_ada, b_ada.reshape(1, N))
    return out[:B]


def _causal_conv(x, buf, w_ref, c0, width, tm):
    buf[HALO:HALO + tm, :] = x
    first = HALO - (CONV_WIDTH - 1)
    acc = w_ref[0:1, c0:c0 + width] * buf[first:first + tm, :]
    for j in range(1, CONV_WIDTH):
        acc = acc + w_ref[j:j + 1, c0:c0 + width] * buf[first + j:first + j + tm, :]
    buf[0:HALO, :] = buf[tm:tm + HALO, :]
    return acc


def _inproj_rg_kernel(x_ref, mod_ref, nw_ref, w_ref, ws_ref, cw_ref, cb_ref, wa_ref, ba_ref, wx_ref, bx_ref, lam_ref,
                      o_ref, os_ref, yrg_ref, xbuf, r_s, i_s, g_s, xc_s, xb_s, hb_s, h_s, *, spb):
    tm = x_ref.shape[0]
    W = cw_ref.shape[2]
    bd = W // RG_BLOCKS
    G = SUBLANES
    n_blocks = w_ref.shape[0]
    n_tail = n_blocks - 4

    @pl.when(pl.program_id(0) % spb == 0)
    def _():
        xbuf[0:HALO, :] = jnp.zeros((HALO, W), F32)
        h_s[...] = jnp.zeros_like(h_s)

    x = x_ref[...]
    y = x * lax.rsqrt(jnp.mean(x * x, axis=-1, keepdims=True) + NORM_EPS) * nw_ref[...]
    hb_s[...] = (y * (1.0 + mod_ref[1:2, :]) + mod_ref[0:1, :]).astype(BF16)

    def proj(n):
        return jnp.dot(hb_s[...], w_ref[n], preferred_element_type=F32)

    xbuf[HALO:HALO + tm, :] = proj(0)
    first = HALO - (CONV_WIDTH - 1)
    for r0 in range(0, tm, 2 * G):
        acc = cw_ref[0] * xbuf[first + r0:first + r0 + 2 * G, :]
        for j in range(1, CONV_WIDTH):
            acc = acc + cw_ref[j] * xbuf[first + j + r0:first + j + r0 + 2 * G, :]
        acc = acc + cb_ref[...]
        xc_s[r0:r0 + 2 * G, :] = acc
        xb_s[r0:r0 + 2 * G, :] = acc.astype(BF16)
    xbuf[0:HALO, :] = xbuf[tm:tm + HALO, :]
    o_ref[0] = proj(2).astype(BF16)
    o_ref[1] = proj(3).astype(BF16)
    for g in range(RG_BLOCKS):
        r_s[:, g * bd:(g + 1) * bd] = jnp.dot(xb_s[:, g * bd:(g + 1) * bd], wa_ref[g], preferred_element_type=F32)
        i_s[:, g * bd:(g + 1) * bd] = jnp.dot(xb_s[:, g * bd:(g + 1) * bd], wx_ref[g], preferred_element_type=F32)
    g_s[...] = _gelu_tanh(proj(1))

    log_a_scale = (-RG_C) * _softplus(-lam_ref[...])
    row = lax.broadcasted_iota(I32, (G, W), 0)
    rows_per_trip = tm // n_tail

    def trip(k, h):
        o_ref[2 + k] = proj(4 + k).astype(BF16)
        for g in range(rows_per_trip // G):
            sl = pl.ds(pl.multiple_of(k * rows_per_trip + g * G, G), G)
            a = jnp.exp(log_a_scale * _sigmoid(r_s[sl, :] + ba_ref[...]))
            b = jnp.sqrt(1.0 - a * a) * (_sigmoid(i_s[sl, :] + bx_ref[...]) * xc_s[sl, :])
            s = 1
            while s < G:
                keep = row >= s
                b = jnp.where(keep, a * pltpu.roll(b, s, 0) + b, b)
                a = jnp.where(keep, a * pltpu.roll(a, s, 0), a)
                s *= 2
            hg = a * h + b
            r_s[sl, :] = hg
            h = jnp.broadcast_to(hg[G - 1:G, :], (G, W))
        return h

    h_s[...] = lax.fori_loop(0, n_tail, trip, h_s[...])
    os_ref[...] = jnp.dot(hb_s[...], ws_ref[...], preferred_element_type=F32)
    yrg_ref[...] = (r_s[...] * g_s[...]).astype(BF16)


def _inproj_rg(x2, mod3, norm_w, w_blocks, w_small, cw, cb, wa, ba, wx, bx, lam, S, tm):
    T, D = x2.shape
    n_blocks, _, W = w_blocks.shape
    spb = S // tm
    bd = W // RG_BLOCKS
    rep = lambda v, n: jnp.broadcast_to(v[..., None, :], v.shape[:-1] + (n, W))
    cw, cb = rep(cw, 2 * SUBLANES), rep(cb, 2 * SUBLANES)
    ba, bx, lam = rep(ba, SUBLANES), rep(bx, SUBLANES), rep(lam, SUBLANES)
    const = lambda shape: pl.BlockSpec(shape, lambda i: (0,) * len(shape))
    return pl.pallas_call(
        functools.partial(_inproj_rg_kernel, spb=spb),
        grid=(T // tm,),
        in_specs=[
            pl.BlockSpec((tm, D), lambda i: (i, 0)),
            pl.BlockSpec((None, 6, D), lambda i: (i // spb, 0, 0)),
            const((1, D)),
            pl.BlockSpec((n_blocks, D, W), lambda i: (0, 0, 0), pipeline_mode=pl.Buffered(1)),
            const((D, LANES)),
            const((CONV_WIDTH, 2 * SUBLANES, W)), const((2 * SUBLANES, W)),
            const((RG_BLOCKS, bd, bd)), const((SUBLANES, W)),
            const((RG_BLOCKS, bd, bd)), const((SUBLANES, W)),
            const((SUBLANES, W)),
        ],
        out_specs=[
            pl.BlockSpec((n_blocks - 2, tm, W), lambda i: (0, i, 0)),
            pl.BlockSpec((tm, LANES), lambda i: (i, 0)),
            pl.BlockSpec((tm, W), lambda i: (i, 0)),
        ],
        out_shape=[jax.ShapeDtypeStruct((n_blocks - 2, T, W), BF16), jax.ShapeDtypeStruct((T, LANES), F32),
                   jax.ShapeDtypeStruct((T, W), BF16)],
        scratch_shapes=[
            pltpu.VMEM((tm + HALO, W), F32),
            pltpu.VMEM((tm, W), F32),
            pltpu.VMEM((tm, W), F32),
            pltpu.VMEM((tm, W), F32),
            pltpu.VMEM((tm, W), F32),
            pltpu.VMEM((tm, W), BF16),
            pltpu.VMEM((tm, D), BF16),
            pltpu.VMEM((SUBLANES, W), F32),
        ],
        compiler_params=_params("arbitrary", limit=VMEM_LIMIT_BIG),
        name="inproj_rg",
    )(x2, mod3, norm_w, w_blocks, w_small, cw, cb, wa, ba, wx, bx, lam)


def _causal_conv_bf16(x_ref, halo, w_ref, c0, shifts_ref):
    tm, width = x_ref.shape
    xb = x_ref[...]
    xf = xb.astype(F32)
    w = lambda j: w_ref[j:j + 1, c0:c0 + width]
    shifted = jnp.dot(shifts_ref[...], xb, preferred_element_type=F32)
    acc = w(0) * shifted[0:tm]
    for j in range(1, CONV_WIDTH - 1):
        acc = acc + w(j) * shifted[j * tm:(j + 1) * tm]
    acc = acc + w(CONV_WIDTH - 1) * xf
    halo[HALO:2 * HALO, :] = xf[0:HALO]
    first = HALO - (CONV_WIDTH - 1)
    head = w(0) * halo[first:first + HALO, :]
    for j in range(1, CONV_WIDTH):
        head = head + w(j) * halo[first + j:first + j + HALO, :]
    halo[0:HALO, :] = xf[tm - HALO:tm]
    return jnp.concatenate([head, acc[HALO:]], axis=0)


def _delta_kernel(q_ref, k_ref, v_ref, z_ref, ab_ref, abt_ref, cw_ref, sh_ref, alr_ref, dtr_ref, alc_ref, dtc_ref, nw_ref,
                  o_ref, qbuf, kbuf, vbuf, q_s, k_s, v_s, o_s, gc_s, gr_s, bt_s, st_s):
    tm, HD = q_ref.shape
    C = DN_CHUNK
    DK = DN_DK
    n_chunks = tm // C

    @pl.when(pl.program_id(1) == 0)
    def _():
        for buf in (qbuf, kbuf, vbuf):
            buf[0:HALO, :] = jnp.zeros((HALO, HD), F32)
        st_s[...] = jnp.zeros_like(st_s)

    q_s[...] = _silu(_causal_conv_bf16(q_ref, qbuf, cw_ref, 0, sh_ref))
    k_s[...] = _silu(_causal_conv_bf16(k_ref, kbuf, cw_ref, HD, sh_ref))
    v_s[...] = _silu(_causal_conv_bf16(v_ref, vbuf, cw_ref, 2 * HD, sh_ref))
    for h in range(DN_HEADS):
        sl = slice(h * DK, (h + 1) * DK)
        qh = q_s[:, sl]
        q_s[:, sl] = qh * lax.rsqrt(jnp.sum(qh * qh, axis=-1, keepdims=True) + NORM_EPS) * (DK ** -0.5)
        kh = k_s[:, sl]
        k_s[:, sl] = kh * lax.rsqrt(jnp.sum(kh * kh, axis=-1, keepdims=True) + NORM_EPS)

    ab = ab_ref[...]
    bt_s[...] = _sigmoid(ab)
    g_rows = -jnp.exp(alr_ref[...]) * _softplus(ab + dtr_ref[...])
    g_cols = -jnp.exp(alc_ref[...]) * _softplus(abt_ref[...] + dtc_ref[...])
    ri = lax.broadcasted_iota(I32, (C, C), 0)
    ci = lax.broadcasted_iota(I32, (C, C), 1)
    causal = ri >= ci
    strict = ri > ci
    lower = causal.astype(F32)
    upper = (ri <= ci).astype(F32)
    for c in range(n_chunks):
        gc_s[c * C:(c + 1) * C, :] = jnp.dot(lower, g_rows[c * C:(c + 1) * C, :], preferred_element_type=F32, precision=HIGHEST)
        gr_s[c] = jnp.dot(g_cols[:, c * C:(c + 1) * C], upper, preferred_element_type=F32, precision=HIGHEST)
    eye = (ri == ci).astype(F32)
    diag_blk = strict & ((ri >> 3) == (ci >> 3))
    merge_masks = []
    sh = 3
    while (1 << sh) < C:
        merge_masks.append(((ri >> (sh + 1)) == (ci >> (sh + 1))) & (((ri >> sh) & 1) == 1) & (((ci >> sh) & 1) == 0))
        sh += 1

    units = [(c, h) for c in range(n_chunks) for h in range(DN_HEADS)]
    ge, kd, qd, rhs, qk, x = {}, {}, {}, {}, {}, {}
    for c in range(n_chunks):
        r0 = c * C
        g_all = gc_s[r0:r0 + C, :]
        g_last = gc_s[r0 + C - 1:r0 + C, :]
        eg_all = jnp.exp(g_all)
        ekd_all = jnp.exp(g_last - g_all)
        ge_all = jnp.exp(g_last)
        beta_all = bt_s[r0:r0 + C, :]
        gr_all = gr_s[c]
        for h in range(DN_HEADS):
            u_ = (c, h)
            sl = slice(h * DK, (h + 1) * DK)
            gl = DN_HEADS + h
            qh = q_s[r0:r0 + C, sl]
            kh = k_s[r0:r0 + C, sl]
            beta = beta_all[:, h:h + 1]
            eg = eg_all[:, gl:gl + 1]
            decay = jnp.exp(jnp.where(causal, g_all[:, gl:gl + 1] - gr_all[gl:gl + 1, :], -jnp.inf))
            kb = kh * beta
            both = _mm_nt(jnp.concatenate([kb, qh], axis=0), kh)
            x[u_] = jnp.where(strict, both[:C] * decay, 0.0)
            qk[u_] = jnp.where(causal, both[C:] * decay, 0.0).astype(BF16)
            rhs[u_] = jnp.concatenate([v_s[r0:r0 + C, sl] * beta, kb * eg], axis=1).astype(BF16)
            qd[u_] = (qh * eg).astype(BF16)
            kd[u_] = (kh * ekd_all[:, gl:gl + 1]).astype(BF16)
            ge[u_] = ge_all[:, gl:gl + 1]
    p = {u_: jnp.where(diag_blk, -x[u_], 0.0) for u_ in units}
    t = {u_: eye + p[u_] for u_ in units}
    for _ in range(2):
        p = {u_: _mm(p[u_], p[u_]) for u_ in units}
        t = {u_: t[u_] + _mm(t[u_], p[u_]) for u_ in units}
    for m in merge_masks:
        y = {u_: _mm(jnp.where(m, x[u_], 0.0), t[u_]) for u_ in units}
        t = {u_: t[u_] - _mm(t[u_], y[u_]) for u_ in units}
    sol = {u_: _mm(t[u_], rhs[u_]) for u_ in units}
    for c in range(n_chunks):
        r0 = c * C
        heads = [(c, h) for h in range(DN_HEADS)]
        st = {u_: st_s[u_[1]] for u_ in heads}
        ws = {u_: _mm(jnp.concatenate([sol[u_][:, DK:], qd[u_]], axis=0), st[u_]) for u_ in heads}
        v_new = {u_: (sol[u_][:, :DK] - ws[u_][:C]).astype(BF16) for u_ in heads}
        for u_ in heads:
            h = u_[1]
            o_s[r0:r0 + C, h * DK:(h + 1) * DK] = ws[u_][C:] + _mm(qk[u_], v_new[u_])
            st_s[h] = st[u_] * ge[u_] + _mm_tn(kd[u_], v_new[u_])

    for h in range(DN_HEADS):
        sl = slice(h * DK, (h + 1) * DK)
        oh = o_s[:, sl]
        y = oh * lax.rsqrt(jnp.mean(oh * oh, axis=-1, keepdims=True) + NORM_EPS) * nw_ref[...]
        o_ref[:, sl] = (y * _silu(z_ref[:, sl].astype(F32))).astype(BF16)


def _delta(proj, ab, abt, cw, alr, dtr, alc, dtc, nw, B, S, tm):
    T = proj.shape[1]
    HD = DN_HEADS * DN_DK
    spb = S // tm
    n_chunks = tm // DN_CHUNK
    col = lambda k: pl.BlockSpec((None, tm, HD), lambda b, j: (k, b * spb + j, 0))
    const = lambda shape: pl.BlockSpec(shape, lambda b, j: (0,) * len(shape))
    t_out = jnp.arange((CONV_WIDTH - 1) * tm, dtype=I32)[:, None]
    shifts = (jnp.arange(tm, dtype=I32)[None, :] == t_out % tm - (CONV_WIDTH - 1 - t_out // tm)).astype(BF16)
    return pl.pallas_call(
        _delta_kernel,
        grid=(B, spb),
        in_specs=[
            col(0), col(1), col(2), col(3),
            pl.BlockSpec((tm, LANES), lambda b, j: (b * spb + j, 0)),
            pl.BlockSpec((None, 2 * DN_HEADS, tm), lambda b, j: (b, 0, j)),
            const((CONV_WIDTH, 3 * HD)),
            const(((CONV_WIDTH - 1) * tm, tm)),
            const((1, LANES)), const((1, LANES)),
            const((2 * DN_HEADS, 1)), const((2 * DN_HEADS, 1)),
            const((1, DN_DK)),
        ],
        out_specs=pl.BlockSpec((tm, HD), lambda b, j: (b * spb + j, 0)),
        out_shape=jax.ShapeDtypeStruct((T, HD), BF16),
        scratch_shapes=[
            pltpu.VMEM((2 * HALO, HD), F32), pltpu.VMEM((2 * HALO, HD), F32), pltpu.VMEM((2 * HALO, HD), F32),
            pltpu.VMEM((tm, HD), F32), pltpu.VMEM((tm, HD), F32), pltpu.VMEM((tm, HD), F32), pltpu.VMEM((tm, HD), F32),
            pltpu.VMEM((tm, LANES), F32),
            pltpu.VMEM((n_chunks, 2 * DN_HEADS, DN_CHUNK), F32),
            pltpu.VMEM((tm, LANES), F32),
            pltpu.VMEM((DN_HEADS, DN_DK, DN_DK), F32),
        ],
        compiler_params=_params("arbitrary", "arbitrary"),
        name="delta",
    )(proj, proj, proj, proj, ab, abt, cw, shifts, alr, dtr, alc, dtc, nw)


def _pack_bf16_pairs(h):
    n = h.shape[1] // 2
    hb = h.astype(BF16).astype(F32)
    lo = lax.bitcast_convert_type(hb[:, :n], U32)
    hi = lax.bitcast_convert_type(hb[:, n:], U32)
    return hi | (lo >> 16)


def _unpack_bf16_pairs(p):
    lo = lax.bitcast_convert_type(p << 16, F32)
    hi = lax.bitcast_convert_type(p & jnp.uint32(0xFFFF0000), F32)
    return jnp.concatenate([lo, hi], axis=1).astype(BF16)


def _merge_kernel(yrg_ref, ydn_ref, grg_ref, gdn_ref, x_ref, mod_ref, wbr_ref, wbd_ref, wo_ref, nw_ref, wr_ref, bias_ref,
                  x1_ref, hp_ref, slab_ref, slabt_ref, cnt_ref, carry, lg_s):
    i = pl.program_id(0)

    @pl.when(i == 0)
    def _():
        carry[...] = jnp.zeros_like(carry)
        lg_s[...] = jnp.zeros_like(lg_s)

    slab = _route_tile(lg_s[...], carry, jnp.where(i > 0, 1.0, 0.0))
    slab_ref[...] = slab
    slabt_ref[...] = jnp.transpose(slab)[0:SUBLANES, :]
    cnt_ref[...] = carry[...]

    m = (_sigmoid(grg_ref[...].astype(F32)) * jnp.dot(yrg_ref[...], wbr_ref[...], preferred_element_type=F32)
         + _sigmoid(gdn_ref[...].astype(F32)) * jnp.dot(ydn_ref[...], wbd_ref[...], preferred_element_type=F32))
    x1 = x_ref[...] + mod_ref[2:3, :] * jnp.dot(m.astype(BF16), wo_ref[...], preferred_element_type=F32)
    x1_ref[...] = x1
    y = x1 * lax.rsqrt(jnp.mean(x1 * x1, axis=-1, keepdims=True) + NORM_EPS) * nw_ref[...]
    h2 = y * (1.0 + mod_ref[4:5, :]) + mod_ref[3:4, :]
    h_hi = h2.astype(BF16)
    h_lo = (h2 - h_hi.astype(F32)).astype(BF16)
    hh = jnp.dot(h_hi, wr_ref[...], preferred_element_type=F32)
    lg_s[...] = (hh[:, :LANES] + (hh[:, LANES:] + jnp.dot(h_lo, wr_ref[:, :LANES], preferred_element_type=F32))) + bias_ref[...]
    hp_ref[...] = _pack_bf16_pairs(h2)


def _merge(y_rg, y_dn, proj, x2, mod3, wbr, wbd, wo, nw, wr, bias, S, tm):
    T, D = x2.shape
    spb = S // tm
    nt = T // tm
    cur = lambda i: jnp.minimum(i, nt - 1)
    prev = lambda i: jnp.maximum(i - 1, 0)
    tile = lambda: pl.BlockSpec((tm, D), lambda i: (cur(i), 0))
    blk = lambda k: pl.BlockSpec((None, tm, D), lambda i: (k, cur(i), 0))
    const = lambda shape: pl.BlockSpec(shape, lambda i: (0,) * len(shape))
    return pl.pallas_call(
        _merge_kernel,
        grid=(nt + 1,),
        in_specs=[
            tile(), tile(), blk(4), blk(5), tile(),
            pl.BlockSpec((None, 6, D), lambda i: (cur(i) // spb, 0, 0)),
            const((D, D)), const((D, D)), const((D, D)), const((1, D)), const((D, 2 * LANES)), const((1, LANES)),
        ],
        out_specs=[tile(), pl.BlockSpec((tm, D // 2), lambda i: (cur(i), 0)),
                   pl.BlockSpec((tm, LANES), lambda i: (prev(i), 0)),
                   pl.BlockSpec((SUBLANES, tm), lambda i: (0, prev(i))), const((1, LANES))],
        out_shape=[jax.ShapeDtypeStruct((T, D), F32), jax.ShapeDtypeStruct((T, D // 2), U32),
                   jax.ShapeDtypeStruct((T, LANES), F32), jax.ShapeDtypeStruct((SUBLANES, T), F32),
                   jax.ShapeDtypeStruct((1, LANES), F32)],
        scratch_shapes=[pltpu.VMEM((1, LANES), F32), pltpu.VMEM((tm, LANES), F32)],
        compiler_params=_params("arbitrary"),
        name="merge",
    )(y_rg, y_dn, proj, proj, x2, mod3, wbr, wbd, wo, nw, wr, bias)


def _route_tile(lg, carry, live):
    tm = lg.shape[0]
    lane = lax.broadcasted_iota(I32, (tm, LANES), 1)
    big = jnp.int32(LANES)
    ninf = -jnp.inf

    def first_max(vals):
        vmax = jnp.max(vals, axis=-1, keepdims=True)
        return vmax, jnp.min(jnp.where(vals == vmax, lane, big), axis=-1, keepdims=True)

    is_g = lane < N_GROUPS
    gmax, gsel = first_max(jnp.where(is_g, lg, ninf))
    p_g = 1.0 / jnp.sum(jnp.where(is_g, jnp.exp(lg - gmax), 0.0), axis=-1, keepdims=True)
    e_lane = lane - N_GROUPS
    in_group = (e_lane >= 0) & (e_lane < N_EXPERTS) & ((e_lane >> 3) == gsel)
    el = jnp.where(in_group, lg, ninf)
    v1, l1 = first_max(el)
    v2, l2 = first_max(jnp.where(lane == l1, ninf, el))
    ex = jnp.exp(v2 - v1)
    w1 = p_g / (1.0 + ex)
    w2 = p_g * ex / (1.0 + ex)
    e1 = l1 - N_GROUPS
    e2 = l2 - N_GROUPS

    hit1 = lane == e1
    hit2 = lane == e2
    onehot = jnp.where(hit1 | hit2, 1.0, 0.0)
    ri = lax.broadcasted_iota(I32, (tm, tm), 0)
    ci = lax.broadcasted_iota(I32, (tm, tm), 1)
    before = jnp.where(ri > ci, 1.0, 0.0).astype(BF16)
    cum = jnp.dot(before, onehot.astype(BF16), preferred_element_type=F32) + carry[...]
    r1 = jnp.sum(jnp.where(hit1, cum, 0.0), axis=-1, keepdims=True)
    r2 = jnp.sum(jnp.where(hit2, cum, 0.0), axis=-1, keepdims=True)
    carry[...] = carry[...] + live * jnp.sum(onehot, axis=0, keepdims=True)

    slab = jnp.where(lane == 0, e1.astype(F32), 0.0)
    slab = jnp.where(lane == 1, e2.astype(F32), slab)
    slab = jnp.where(lane == 2, r1, slab)
    slab = jnp.where(lane == 3, r2, slab)
    slab = jnp.where(lane == 4, w1, slab)
    return jnp.where(lane == 5, w2, slab)


def _row_copy(src, dst, s, d, sem):
    return pltpu.make_async_copy(src.at[pl.ds(s, 1), :], dst.at[pl.ds(d, 1), :], sem)


def _sort_kernel(dest_ref, hp_ref, xs_in_ref, xs_ref, sem, *, tb):
    del xs_in_ref

    for r in range(tb):
        _row_copy(hp_ref, xs_ref, r, dest_ref[0, r], sem).start(priority=0)
        _row_copy(hp_ref, xs_ref, r, dest_ref[0, tb + r], sem).start(priority=1)

    def drain(r, carry):
        _row_copy(hp_ref, xs_ref, 0, 0, sem).wait()
        _row_copy(hp_ref, xs_ref, 0, 0, sem).wait()
        return carry

    lax.fori_loop(0, tb, drain, 0, unroll=8)


def _sort(dest3, hp, xs0, tb):
    T, Dh = hp.shape
    return pl.pallas_call(
        functools.partial(_sort_kernel, tb=tb),
        grid=(T // tb,),
        in_specs=[
            pl.BlockSpec((None, 1, 2 * tb), lambda i: (i, 0, 0), memory_space=pltpu.SMEM),
            pl.BlockSpec((tb, Dh), lambda i: (i, 0)),
            pl.BlockSpec(memory_space=pl.ANY),
        ],
        out_specs=pl.BlockSpec(memory_space=pl.ANY),
        out_shape=jax.ShapeDtypeStruct(xs0.shape, xs0.dtype),
        scratch_shapes=[pltpu.SemaphoreType.DMA],
        input_output_aliases={2: 0},
        compiler_params=pltpu.CompilerParams(dimension_semantics=("arbitrary",), has_side_effects=True),
        name="sort",
    )(dest3, hp, xs0)


def _experts_kernel(be_ref, nv_ref, x_ref, wg_ref, wu_ref, wd_ref, y_ref, wg_b, wu_b, wd_b):
    i = pl.program_id(0)
    valid = i < nv_ref[0]

    @pl.when(valid & ((i == 0) | (be_ref[i] != be_ref[jnp.maximum(i - 1, 0)])))
    def _():
        wg_b[...] = wg_ref[...].astype(BF16)
        wu_b[...] = wu_ref[...].astype(BF16)
        wd_b[...] = wd_ref[...].astype(BF16)

    @pl.when(valid)
    def _():
        x = _unpack_bf16_pairs(x_ref[...])
        g = jnp.dot(x, wg_b[...], preferred_element_type=F32)
        u = jnp.dot(x, wu_b[...], preferred_element_type=F32)
        y_ref[...] = jnp.dot((_silu(g) * u).astype(BF16), wd_b[...], preferred_element_type=F32)

    @pl.when(jnp.logical_not(valid))
    def _():
        y_ref[...] = jnp.zeros_like(y_ref)


def _experts(blk_e, nvb, xs, wg, wu, wd):
    P, Dh = xs.shape
    E, D, DE = wg.shape
    NB = P // MOE_BLOCK
    last = lambda i, nv: jnp.minimum(i, nv[0] - 1)
    grid_spec = pltpu.PrefetchScalarGridSpec(
        num_scalar_prefetch=2,
        grid=(NB,),
        in_specs=[
            pl.BlockSpec((MOE_BLOCK, Dh), lambda i, be, nv: (last(i, nv), 0)),
            pl.BlockSpec((None, D, DE), lambda i, be, nv: (be[last(i, nv)], 0, 0)),
            pl.BlockSpec((None, D, DE), lambda i, be, nv: (be[last(i, nv)], 0, 0)),
            pl.BlockSpec((None, DE, D), lambda i, be, nv: (be[last(i, nv)], 0, 0)),
        ],
        out_specs=pl.BlockSpec((MOE_BLOCK, D), lambda i, be, nv: (i, 0)),
        scratch_shapes=[pltpu.VMEM((D, DE), BF16), pltpu.VMEM((D, DE), BF16), pltpu.VMEM((DE, D), BF16)],
    )
    return pl.pallas_call(
        _experts_kernel,
        grid_spec=grid_spec,
        out_shape=jax.ShapeDtypeStruct((P, D), F32),
        compiler_params=_params("arbitrary"),
        name="experts",
    )(blk_e, nvb, xs, wg, wu, wd)


def _final_kernel(dest_ref, next_ref, x1_ref, slab_ref, mod_ref, fw_ref, ys_ref, o_ref, ybuf, sems, *, tb):
    i = pl.program_id(0)
    nt = pl.num_programs(0)

    def gather(idx_ref, slot):
        for r in range(tb):
            _row_copy(ys_ref, ybuf.at[slot, 0], idx_ref[0, r], r, sems.at[slot]).start(priority=0)
            _row_copy(ys_ref, ybuf.at[slot, 1], idx_ref[0, tb + r], r, sems.at[slot]).start(priority=1)

    def step(slot):
        @pl.when(i + 1 < nt)
        def _():
            gather(next_ref, 1 - slot)

        def drain(r, carry):
            _row_copy(ys_ref, ybuf.at[slot, 0], 0, 0, sems.at[slot]).wait()
            _row_copy(ys_ref, ybuf.at[slot, 1], 0, 0, sems.at[slot]).wait()
            return carry

        lax.fori_loop(0, tb, drain, 0, unroll=8)

        slab = slab_ref[...]
        y = slab[:, 4:5] * ybuf[slot, 0] + slab[:, 5:6] * ybuf[slot, 1]
        x2 = x1_ref[...] + mod_ref[5:6, :] * y
        o_ref[...] = x2 * lax.rsqrt(jnp.mean(x2 * x2, axis=-1, keepdims=True) + NORM_EPS) * fw_ref[...]

    @pl.when(i == 0)
    def _():
        gather(dest_ref, 0)

    for slot in range(2):
        pl.when(i % 2 == slot)(functools.partial(step, slot))


def _final(dest3, x1, slab, mod3, fw, ys, S, tb):
    T, D = x1.shape
    spb = S // tb
    nt = T // tb
    return pl.pallas_call(
        functools.partial(_final_kernel, tb=tb),
        grid=(nt,),
        in_specs=[
            pl.BlockSpec((None, 1, 2 * tb), lambda i: (i, 0, 0), memory_space=pltpu.SMEM),
            pl.BlockSpec((None, 1, 2 * tb), lambda i: (jnp.minimum(i + 1, nt - 1), 0, 0), memory_space=pltpu.SMEM),
            pl.BlockSpec((tb, D), lambda i: (i, 0)),
            pl.BlockSpec((tb, LANES), lambda i: (i, 0)),
            pl.BlockSpec((None, 6, D), lambda i: (i // spb, 0, 0)),
            pl.BlockSpec((1, D), lambda i: (0, 0)),
            pl.BlockSpec(memory_space=pl.ANY),
        ],
        out_specs=pl.BlockSpec((tb, D), lambda i: (i, 0)),
        out_shape=jax.ShapeDtypeStruct((T, D), F32),
        scratch_shapes=[pltpu.VMEM((2, 2, tb, D), F32), pltpu.SemaphoreType.DMA((2,))],
        compiler_params=_params("arbitrary"),
        name="final",
    )(dest3, dest3, x1, slab, mod3, fw, ys)


def _lane_row(vals, offset):
    return jnp.zeros((1, LANES), F32).at[0, offset:offset + vals.shape[0]].set(vals.astype(F32))


def _forward(x, c, w_ada, b_ada, norm1_w, w_in, rg_conv_w, rg_conv_b, rg_gate_a_w, rg_gate_a_b, rg_gate_x_w, rg_gate_x_b,
             rg_lambda, dn_conv_w, dn_a_log, dn_dt_bias, dn_norm_w, w_branch_rg, w_branch_dn, w_out, norm2_w,
             moe_w_group, moe_b_group, moe_w_router, moe_b_router, moe_w_gate, moe_w_up, moe_w_down, final_norm_w,
             tm_proj=512, tm_delta=256, tm_tok=512):
    B, S, D = x.shape
    T = B * S
    H = DN_HEADS
    x2 = x.reshape(T, D)
    mod3 = _ada(c, w_ada[0], b_ada[0]).reshape(B, 6, D)

    w = w_in[0]
    n_wide = 6 * D
    col0s = [n * D for n in range(6)] + [n_wide + 2 * H, n_wide + 2 * H + D]
    w_blocks = jnp.stack([w[:, c0:c0 + D] for c0 in col0s]).astype(BF16)
    w_small = jnp.pad(w[:, n_wide:n_wide + 2 * H], ((0, 0), (0, LANES - 2 * H))).astype(BF16)
    proj, ab, y_rg = _inproj_rg(x2, mod3, norm1_w, w_blocks, w_small, rg_conv_w[0], rg_conv_b[0],
                                rg_gate_a_w[0].astype(BF16), rg_gate_a_b.reshape(-1),
                                rg_gate_x_w[0].astype(BF16), rg_gate_x_b.reshape(-1), rg_lambda[0], S, tm_proj)

    abt = ab[:, :2 * H].reshape(B, S, 2 * H).transpose(0, 2, 1)
    zeros_h = jnp.zeros((H,), F32)
    col = lambda v: jnp.concatenate([zeros_h, v.astype(F32)]).reshape(2 * H, 1)
    y_dn = _delta(proj, ab, abt, dn_conv_w[0], _lane_row(dn_a_log[0], H), _lane_row(dn_dt_bias[0], H),
                  col(dn_a_log[0]), col(dn_dt_bias[0]), dn_norm_w, B, S, tm_delta)

    w_route = jnp.pad(jnp.concatenate([moe_w_group[0], moe_w_router[0]], axis=1), ((0, 0), (0, LANES - N_GROUPS - N_EXPERTS)))
    w_route_hi = w_route.astype(BF16)
    w_route = jnp.concatenate([w_route_hi, (w_route - w_route_hi.astype(F32)).astype(BF16)], axis=1)
    bias = _lane_row(jnp.concatenate([moe_b_group[0], moe_b_router[0]]), 0)
    x1, hp, slab, slabt, cnt = _merge(y_rg, y_dn, proj, x2, mod3, w_branch_rg[0].astype(BF16),
                                      w_branch_dn[0].astype(BF16), w_out[0].astype(BF16), norm2_w, w_route, bias, S, tm_proj)

    eid = slabt[0:2].astype(I32)
    rank = slabt[2:4].astype(I32)
    counts = cnt[0, :N_EXPERTS].astype(I32)
    padded = (counts + MOE_BLOCK - 1) // MOE_BLOCK * MOE_BLOCK
    pad_end = jnp.cumsum(padded)
    pad_start = pad_end - padded
    seg = jnp.sum(jnp.where(eid[:, :, None] == jnp.arange(N_EXPERTS, dtype=I32), pad_start, 0), axis=-1)
    dest = seg + rank
    nt = T // tm_tok
    dest3 = dest.reshape(2, nt, tm_tok).transpose(1, 0, 2).reshape(nt, 1, 2 * tm_tok)
    A = 2 * T
    P = (A + N_EXPERTS * (MOE_BLOCK - 1) + MOE_BLOCK - 1) // MOE_BLOCK * MOE_BLOCK
    NB = P // MOE_BLOCK
    starts = jnp.arange(NB, dtype=I32) * MOE_BLOCK
    blk_e = jnp.minimum(jnp.sum(pad_end[None, :] <= starts[:, None], axis=1), N_EXPERTS - 1).astype(I32)
    nvb = (pad_end[-1:] // MOE_BLOCK).astype(I32)

    xs = _sort(dest3, hp, jnp.zeros((P, D // 2), U32), tm_tok)
    ys = _experts(blk_e, nvb, xs, moe_w_gate[0], moe_w_up[0], moe_w_down[0])
    out = _final(dest3, x1, slab, mod3, final_norm_w.reshape(1, D), ys, S, tm_tok)
    return out.reshape(B, S, D)


def kernel(x, c, w_ada, b_ada, norm1_w, w_in, rg_conv_w, rg_conv_b, rg_gate_a_w, rg_gate_a_b, rg_gate_x_w, rg_gate_x_b, rg_lambda, dn_conv_w, dn_a_log, dn_dt_bias, dn_norm_w, w_branch_rg, w_branch_dn, w_out, norm2_w, moe_w_group, moe_b_group, moe_w_router, moe_b_router, moe_w_gate, moe_w_up, moe_w_down, final_norm_w):
    return _forward(x, c, w_ada, b_ada, norm1_w, w_in, rg_conv_w, rg_conv_b, rg_gate_a_w, rg_gate_a_b, rg_gate_x_w, rg_gate_x_b, rg_lambda, dn_conv_w, dn_a_log, dn_dt_bias, dn_norm_w, w_branch_rg, w_branch_dn, w_out, norm2_w, moe_w_group, moe_b_group, moe_w_router, moe_b_router, moe_w_gate, moe_w_up, moe_w_down, final_norm_w)
```

```python
import functools

import jax
import jax.numpy as jnp
from jax import lax
from jax.experimental import pallas as pl
from jax.experimental.pallas import tpu as pltpu

F32 = jnp.float32
BF16 = jnp.bfloat16
U32 = jnp.uint32
I32 = jnp.int32
HIGHEST = lax.Precision.HIGHEST

NORM_EPS = 1e-6
RG_C = 8.0
RG_BLOCKS = 4
CONV_WIDTH = 4
DN_HEADS = 8
DN_DK = 128
DN_CHUNK = 64
N_GROUPS = 4
EXPERTS_PER_GROUP = 8
N_EXPERTS = N_GROUPS * EXPERTS_PER_GROUP
MOE_BLOCK = 512
LANES = 128
SUBLANES = 8
HALO = SUBLANES
VMEM_LIMIT = 48 * 1024 * 1024
VMEM_LIMIT_BIG = 56 * 1024 * 1024


def _sigmoid(x):
    return 1.0 / (1.0 + jnp.exp(-x))


def _silu(x):
    return x * _sigmoid(x)


def _softplus(x):
    return jnp.maximum(x, 0.0) + jnp.log(1.0 + jnp.exp(-jnp.abs(x)))


def _gelu_tanh(x):
    return x * (0.5 * (1.0 + jnp.tanh(0.7978845608028654 * (x + 0.044715 * (x * x * x)))))


def _mm(a, b):
    return jnp.dot(a.astype(BF16), b.astype(BF16), preferred_element_type=F32)


def _mm_nt(a, b):
    return lax.dot_general(a.astype(BF16), b.astype(BF16), (((1,), (1,)), ((), ())), preferred_element_type=F32)


def _mm_tn(a, b):
    return lax.dot_general(a.astype(BF16), b.astype(BF16), (((0,), (0,)), ((), ())), preferred_element_type=F32)


def _params(*sem, limit=None):
    return pltpu.CompilerParams(dimension_semantics=sem, vmem_limit_bytes=limit or VMEM_LIMIT)


def _ada_kernel(c_ref, w_ref, b_ref, o_ref):
    o_ref[...] = jnp.dot(_silu(c_ref[...]), w_ref[...], preferred_element_type=F32, precision=HIGHEST) + b_ref[...]


def _ada(c, w_ada, b_ada):
    B, D = c.shape
    N = w_ada.shape[1]
    rows = -(-B // SUBLANES) * SUBLANES
    cp = jnp.pad(c, ((0, rows - B), (0, 0)))
    tn = 1536
    out = pl.pallas_call(
        _ada_kernel,
        grid=(N // tn,),
        in_specs=[
            pl.BlockSpec((rows, D), lambda n: (0, 0)),
            pl.BlockSpec((D, tn), lambda n: (0, n)),
            pl.BlockSpec((1, tn), lambda n: (0, n)),
        ],
        out_specs=pl.BlockSpec((rows, tn), lambda n: (0, n)),
        out_shape=jax.ShapeDtypeStruct((rows, N), F32),
        compiler_params=_params("arbitrary"),
        name="ada",
    )(cp, w_ada, b_ada.reshape(1, N))
    return out[:B]


def _causal_conv(x, buf, w_ref, c0, width, tm):
    buf[HALO:HALO + tm, :] = x
    first = HALO - (CONV_WIDTH - 1)
    acc = w_ref[0:1, c0:c0 + width] * buf[first:first + tm, :]
    for j in range(1, CONV_WIDTH):
        acc = acc + w_ref[j:j + 1, c0:c0 + width] * buf[first + j:first + j + tm, :]
    buf[0:HALO, :] = buf[tm:tm + HALO, :]
    return acc


def _inproj_rg_kernel(x_ref, mod_ref, nw_ref, w_ref, ws_ref, cw_ref, cb_ref, wa_ref, ba_ref, wx_ref, bx_ref, lam_ref,
                      o_ref, os_ref, ost_ref, yrg_ref, xbuf, r_s, i_s, g_s, xc_s, xb_s, hb_s, h_s, *, spb):
    tm = x_ref.shape[0]
    W = cw_ref.shape[2]
    bd = W // RG_BLOCKS
    G = SUBLANES
    n_blocks = w_ref.shape[0]
    n_tail = n_blocks - 4

    @pl.when(pl.program_id(0) % spb == 0)
    def _():
        xbuf[0:HALO, :] = jnp.zeros((HALO, W), F32)
        h_s[...] = jnp.zeros_like(h_s)

    x = x_ref[...]
    y = x * lax.rsqrt(jnp.mean(x * x, axis=-1, keepdims=True) + NORM_EPS) * nw_ref[...]
    hb_s[...] = (y * (1.0 + mod_ref[1:2, :]) + mod_ref[0:1, :]).astype(BF16)

    def proj(n):
        return jnp.dot(hb_s[...], w_ref[n], preferred_element_type=F32)

    xbuf[HALO:HALO + tm, :] = proj(0)
    first = HALO - (CONV_WIDTH - 1)
    for r0 in range(0, tm, 2 * G):
        acc = cw_ref[0] * xbuf[first + r0:first + r0 + 2 * G, :]
        for j in range(1, CONV_WIDTH):
            acc = acc + cw_ref[j] * xbuf[first + j + r0:first + j + r0 + 2 * G, :]
        acc = acc + cb_ref[...]
        xc_s[r0:r0 + 2 * G, :] = acc
        xb_s[r0:r0 + 2 * G, :] = acc.astype(BF16)
    xbuf[0:HALO, :] = xbuf[tm:tm + HALO, :]
    o_ref[0] = proj(2).astype(BF16)
    o_ref[1] = proj(3).astype(BF16)
    for g in range(RG_BLOCKS):
        r_s[:, g * bd:(g + 1) * bd] = jnp.dot(xb_s[:, g * bd:(g + 1) * bd], wa_ref[g], preferred_element_type=F32)
        i_s[:, g * bd:(g + 1) * bd] = jnp.dot(xb_s[:, g * bd:(g + 1) * bd], wx_ref[g], preferred_element_type=F32)
    g_s[...] = _gelu_tanh(proj(1))

    log_a_scale = (-RG_C) * _softplus(-lam_ref[...])
    row = lax.broadcasted_iota(I32, (G, W), 0)
    rows_per_trip = tm // n_tail

    def trip(k, h):
        o_ref[2 + k] = proj(4 + k).astype(BF16)
        for g in range(rows_per_trip // G):
            sl = pl.ds(pl.multiple_of(k * rows_per_trip + g * G, G), G)
            a = jnp.exp(log_a_scale * _sigmoid(r_s[sl, :] + ba_ref[...]))
            b = jnp.sqrt(1.0 - a * a) * (_sigmoid(i_s[sl, :] + bx_ref[...]) * xc_s[sl, :])
            s = 1
            while s < G:
                keep = row >= s
                b = jnp.where(keep, a * pltpu.roll(b, s, 0) + b, b)
                a = jnp.where(keep, a * pltpu.roll(a, s, 0), a)
                s *= 2
            hg = a * h + b
            r_s[sl, :] = hg
            h = jnp.broadcast_to(hg[G - 1:G, :], (G, W))
        return h

    h_s[...] = lax.fori_loop(0, n_tail, trip, h_s[...])
    small = jnp.dot(hb_s[...], ws_ref[...], preferred_element_type=F32)
    os_ref[...] = small
    ost_ref[...] = jnp.transpose(small)[0:ost_ref.shape[0], :]
    yrg_ref[...] = (r_s[...] * g_s[...]).astype(BF16)


def _inproj_rg(x2, mod3, norm_w, w_blocks, w_small, cw, cb, wa, ba, wx, bx, lam, S, tm):
    T, D = x2.shape
    n_blocks, _, W = w_blocks.shape
    spb = S // tm
    bd = W // RG_BLOCKS
    rep = lambda v, n: jnp.broadcast_to(v[..., None, :], v.shape[:-1] + (n, W))
    cw, cb = rep(cw, 2 * SUBLANES), rep(cb, 2 * SUBLANES)
    ba, bx, lam = rep(ba, SUBLANES), rep(bx, SUBLANES), rep(lam, SUBLANES)
    const = lambda shape: pl.BlockSpec(shape, lambda i: (0,) * len(shape))
    return pl.pallas_call(
        functools.partial(_inproj_rg_kernel, spb=spb),
        grid=(T // tm,),
        in_specs=[
            pl.BlockSpec((tm, D), lambda i: (i, 0)),
            pl.BlockSpec((None, 6, D), lambda i: (i // spb, 0, 0)),
            const((1, D)),
            pl.BlockSpec((n_blocks, D, W), lambda i: (0, 0, 0), pipeline_mode=pl.Buffered(1)),
            const((D, LANES)),
            const((CONV_WIDTH, 2 * SUBLANES, W)), const((2 * SUBLANES, W)),
            const((RG_BLOCKS, bd, bd)), const((SUBLANES, W)),
            const((RG_BLOCKS, bd, bd)), const((SUBLANES, W)),
            const((SUBLANES, W)),
        ],
        out_specs=[
            pl.BlockSpec((n_blocks - 2, tm, W), lambda i: (0, i, 0)),
            pl.BlockSpec((tm, LANES), lambda i: (i, 0)),
            pl.BlockSpec((None, 2 * DN_HEADS, tm), lambda i: (i // spb, 0, i % spb)),
            pl.BlockSpec((tm, W), lambda i: (i, 0)),
        ],
        out_shape=[jax.ShapeDtypeStruct((n_blocks - 2, T, W), BF16), jax.ShapeDtypeStruct((T, LANES), F32),
                   jax.ShapeDtypeStruct((T // S, 2 * DN_HEADS, S), F32), jax.ShapeDtypeStruct((T, W), BF16)],
        scratch_shapes=[
            pltpu.VMEM((tm + HALO, W), F32),
            pltpu.VMEM((tm, W), F32),
            pltpu.VMEM((tm, W), F32),
            pltpu.VMEM((tm, W), F32),
            pltpu.VMEM((tm, W), F32),
            pltpu.VMEM((tm, W), BF16),
            pltpu.VMEM((tm, D), BF16),
            pltpu.VMEM((SUBLANES, W), F32),
        ],
        compiler_params=_params("arbitrary", limit=VMEM_LIMIT_BIG),
        name="inproj_rg",
    )(x2, mod3, norm_w, w_blocks, w_small, cw, cb, wa, ba, wx, bx, lam)


def _causal_conv_bf16(x_ref, halo, w_ref, c0, shifts_ref):
    tm, width = x_ref.shape
    xb = x_ref[...]
    xf = xb.astype(F32)
    w = lambda j: w_ref[j:j + 1, c0:c0 + width]
    shifted = jnp.dot(shifts_ref[...], xb, preferred_element_type=F32)
    acc = w(0) * shifted[0:tm]
    for j in range(1, CONV_WIDTH - 1):
        acc = acc + w(j) * shifted[j * tm:(j + 1) * tm]
    acc = acc + w(CONV_WIDTH - 1) * xf
    halo[HALO:2 * HALO, :] = xf[0:HALO]
    first = HALO - (CONV_WIDTH - 1)
    head = w(0) * halo[first:first + HALO, :]
    for j in range(1, CONV_WIDTH):
        head = head + w(j) * halo[first + j:first + j + HALO, :]
    halo[0:HALO, :] = xf[tm - HALO:tm]
    return jnp.concatenate([head, acc[HALO:]], axis=0)


def _delta_kernel(q_ref, k_ref, v_ref, z_ref, ab_ref, abt_ref, cw_ref, sh_ref, alr_ref, dtr_ref, alc_ref, dtc_ref, nw_ref,
                  o_ref, qbuf, kbuf, vbuf, q_s, k_s, v_s, o_s, gc_s, gr_s, bt_s, st_s):
    tm, HD = q_ref.shape
    C = DN_CHUNK
    DK = DN_DK
    n_chunks = tm // C

    @pl.when(pl.program_id(1) == 0)
    def _():
        for buf in (qbuf, kbuf, vbuf):
            buf[0:HALO, :] = jnp.zeros((HALO, HD), F32)
        st_s[...] = jnp.zeros_like(st_s)

    q_s[...] = _silu(_causal_conv_bf16(q_ref, qbuf, cw_ref, 0, sh_ref))
    k_s[...] = _silu(_causal_conv_bf16(k_ref, kbuf, cw_ref, HD, sh_ref))
    v_s[...] = _silu(_causal_conv_bf16(v_ref, vbuf, cw_ref, 2 * HD, sh_ref))
    for h in range(DN_HEADS):
        sl = slice(h * DK, (h + 1) * DK)
        qh = q_s[:, sl]
        q_s[:, sl] = qh * lax.rsqrt(jnp.sum(qh * qh, axis=-1, keepdims=True) + NORM_EPS) * (DK ** -0.5)
        kh = k_s[:, sl]
        k_s[:, sl] = kh * lax.rsqrt(jnp.sum(kh * kh, axis=-1, keepdims=True) + NORM_EPS)

    ab = ab_ref[...]
    bt_s[...] = _sigmoid(ab)
    g_rows = -jnp.exp(alr_ref[...]) * _softplus(ab + dtr_ref[...])
    g_cols = -jnp.exp(alc_ref[...]) * _softplus(abt_ref[...] + dtc_ref[...])
    ri = lax.broadcasted_iota(I32, (C, C), 0)
    ci = lax.broadcasted_iota(I32, (C, C), 1)
    causal = ri >= ci
    strict = ri > ci
    lower = causal.astype(F32)
    upper = (ri <= ci).astype(F32)
    for c in range(n_chunks):
        gc_s[c * C:(c + 1) * C, :] = jnp.dot(lower, g_rows[c * C:(c + 1) * C, :], preferred_element_type=F32, precision=HIGHEST)
        gr_s[c] = jnp.dot(g_cols[:, c * C:(c + 1) * C], upper, preferred_element_type=F32, precision=HIGHEST)
    eye = (ri == ci).astype(F32)
    diag_blk = strict & ((ri >> 3) == (ci >> 3))
    merge_masks = []
    sh = 3
    while (1 << sh) < C:
        merge_masks.append(((ri >> (sh + 1)) == (ci >> (sh + 1))) & (((ri >> sh) & 1) == 1) & (((ci >> sh) & 1) == 0))
        sh += 1

    units = [(c, h) for c in range(n_chunks) for h in range(DN_HEADS)]
    ge, kd, qd, rhs, qk, x = {}, {}, {}, {}, {}, {}
    for c in range(n_chunks):
        r0 = c * C
        g_all = gc_s[r0:r0 + C, :]
        g_last = gc_s[r0 + C - 1:r0 + C, :]
        eg_all = jnp.exp(g_all)
        ekd_all = jnp.exp(g_last - g_all)
        ge_all = jnp.exp(g_last)
        beta_all = bt_s[r0:r0 + C, :]
        gr_all = gr_s[c]
        for h in range(DN_HEADS):
            u_ = (c, h)
            sl = slice(h * DK, (h + 1) * DK)
            gl = DN_HEADS + h
            qh = q_s[r0:r0 + C, sl]
            kh = k_s[r0:r0 + C, sl]
            beta = beta_all[:, h:h + 1]
            eg = eg_all[:, gl:gl + 1]
            decay = jnp.exp(jnp.where(causal, g_all[:, gl:gl + 1] - gr_all[gl:gl + 1, :], -jnp.inf))
            kb = kh * beta
            both = _mm_nt(jnp.concatenate([kb, qh], axis=0), kh)
            x[u_] = jnp.where(strict, both[:C] * decay, 0.0)
            qk[u_] = jnp.where(causal, both[C:] * decay, 0.0).astype(BF16)
            rhs[u_] = jnp.concatenate([v_s[r0:r0 + C, sl] * beta, kb * eg], axis=1).astype(BF16)
            qd[u_] = (qh * eg).astype(BF16)
            kd[u_] = (kh * ekd_all[:, gl:gl + 1]).astype(BF16)
            ge[u_] = ge_all[:, gl:gl + 1]
    p = {u_: jnp.where(diag_blk, -x[u_], 0.0) for u_ in units}
    t = {u_: eye + p[u_] for u_ in units}
    for _ in range(2):
        p = {u_: _mm(p[u_], p[u_]) for u_ in units}
        t = {u_: t[u_] + _mm(t[u_], p[u_]) for u_ in units}
    for m in merge_masks:
        y = {u_: _mm(jnp.where(m, x[u_], 0.0), t[u_]) for u_ in units}
        t = {u_: t[u_] - _mm(t[u_], y[u_]) for u_ in units}
    sol = {u_: _mm(t[u_], rhs[u_]) for u_ in units}
    for c in range(n_chunks):
        r0 = c * C
        heads = [(c, h) for h in range(DN_HEADS)]
        st = {u_: st_s[u_[1]] for u_ in heads}
        ws = {u_: _mm(jnp.concatenate([sol[u_][:, DK:], qd[u_]], axis=0), st[u_]) for u_ in heads}
        v_new = {u_: (sol[u_][:, :DK] - ws[u_][:C]).astype(BF16) for u_ in heads}
        for u_ in heads:
            h = u_[1]
            o_s[r0:r0 + C, h * DK:(h + 1) * DK] = ws[u_][C:] + _mm(qk[u_], v_new[u_])
            st_s[h] = st[u_] * ge[u_] + _mm_tn(kd[u_], v_new[u_])

    for h in range(DN_HEADS):
        sl = slice(h * DK, (h + 1) * DK)
        oh = o_s[:, sl]
        y = oh * lax.rsqrt(jnp.mean(oh * oh, axis=-1, keepdims=True) + NORM_EPS) * nw_ref[...]
        o_ref[:, sl] = (y * _silu(z_ref[:, sl].astype(F32))).astype(BF16)


def _delta(proj, ab, abt, cw, alr, dtr, alc, dtc, nw, B, S, tm):
    T = proj.shape[1]
    HD = DN_HEADS * DN_DK
    spb = S // tm
    n_chunks = tm // DN_CHUNK
    col = lambda k: pl.BlockSpec((None, tm, HD), lambda b, j: (k, b * spb + j, 0))
    const = lambda shape: pl.BlockSpec(shape, lambda b, j: (0,) * len(shape))
    t_out = jnp.arange((CONV_WIDTH - 1) * tm, dtype=I32)[:, None]
    shifts = (jnp.arange(tm, dtype=I32)[None, :] == t_out % tm - (CONV_WIDTH - 1 - t_out // tm)).astype(BF16)
    return pl.pallas_call(
        _delta_kernel,
        grid=(B, spb),
        in_specs=[
            col(0), col(1), col(2), col(3),
            pl.BlockSpec((tm, LANES), lambda b, j: (b * spb + j, 0)),
            pl.BlockSpec((None, 2 * DN_HEADS, tm), lambda b, j: (b, 0, j)),
            const((CONV_WIDTH, 3 * HD)),
            const(((CONV_WIDTH - 1) * tm, tm)),
            const((1, LANES)), const((1, LANES)),
            const((2 * DN_HEADS, 1)), const((2 * DN_HEADS, 1)),
            const((1, DN_DK)),
        ],
        out_specs=pl.BlockSpec((tm, HD), lambda b, j: (b * spb + j, 0)),
        out_shape=jax.ShapeDtypeStruct((T, HD), BF16),
        scratch_shapes=[
            pltpu.VMEM((2 * HALO, HD), F32), pltpu.VMEM((2 * HALO, HD), F32), pltpu.VMEM((2 * HALO, HD), F32),
            pltpu.VMEM((tm, HD), F32), pltpu.VMEM((tm, HD), F32), pltpu.VMEM((tm, HD), F32), pltpu.VMEM((tm, HD), F32),
            pltpu.VMEM((tm, LANES), F32),
            pltpu.VMEM((n_chunks, 2 * DN_HEADS, DN_CHUNK), F32),
            pltpu.VMEM((tm, LANES), F32),
            pltpu.VMEM((DN_HEADS, DN_DK, DN_DK), F32),
        ],
        compiler_params=_params("arbitrary", "arbitrary"),
        name="delta",
    )(proj, proj, proj, proj, ab, abt, cw, shifts, alr, dtr, alc, dtc, nw)


def _pack_bf16_pairs(h):
    n = h.shape[1] // 2
    hb = h.astype(BF16).astype(F32)
    lo = lax.bitcast_convert_type(hb[:, :n], U32)
    hi = lax.bitcast_convert_type(hb[:, n:], U32)
    return hi | (lo >> 16)


def _unpack_bf16_pairs(p):
    lo = lax.bitcast_convert_type(p << 16, F32)
    hi = lax.bitcast_convert_type(p & jnp.uint32(0xFFFF0000), F32)
    return jnp.concatenate([lo, hi], axis=1).astype(BF16)


def _merge_kernel(yrg_ref, ydn_ref, grg_ref, gdn_ref, x_ref, mod_ref, wbr_ref, wbd_ref, wo_ref, nw_ref, wr_ref, bias_ref,
                  x1_ref, hp_ref, slab_ref, slabt_ref, cnt_ref, carry, lg_s):
    i = pl.program_id(0)

    @pl.when(i == 0)
    def _():
        carry[...] = jnp.zeros_like(carry)
        lg_s[...] = jnp.zeros_like(lg_s)

    slab = _route_tile(lg_s[...], carry, jnp.where(i > 0, 1.0, 0.0))
    slab_ref[...] = slab
    slabt_ref[...] = jnp.transpose(slab)[0:SUBLANES, :]
    cnt_ref[...] = carry[...]

    m = (_sigmoid(grg_ref[...].astype(F32)) * jnp.dot(yrg_ref[...], wbr_ref[...], preferred_element_type=F32)
         + _sigmoid(gdn_ref[...].astype(F32)) * jnp.dot(ydn_ref[...], wbd_ref[...], preferred_element_type=F32))
    x1 = x_ref[...] + mod_ref[2:3, :] * jnp.dot(m.astype(BF16), wo_ref[...], preferred_element_type=F32)
    x1_ref[...] = x1
    y = x1 * lax.rsqrt(jnp.mean(x1 * x1, axis=-1, keepdims=True) + NORM_EPS) * nw_ref[...]
    h2 = y * (1.0 + mod_ref[4:5, :]) + mod_ref[3:4, :]
    h_hi = h2.astype(BF16)
    h_lo = (h2 - h_hi.astype(F32)).astype(BF16)
    hh = jnp.dot(h_hi, wr_ref[...], preferred_element_type=F32)
    lg_s[...] = (hh[:, :LANES] + (hh[:, LANES:] + jnp.dot(h_lo, wr_ref[:, :LANES], preferred_element_type=F32))) + bias_ref[...]
    hp_ref[...] = _pack_bf16_pairs(h2)


def _merge(y_rg, y_dn, proj, x2, mod3, wbr, wbd, wo, nw, wr, bias, S, tm):
    T, D = x2.shape
    spb = S // tm
    nt = T // tm
    cur = lambda i: jnp.minimum(i, nt - 1)
    prev = lambda i: jnp.maximum(i - 1, 0)
    tile = lambda: pl.BlockSpec((tm, D), lambda i: (cur(i), 0))
    blk = lambda k: pl.BlockSpec((None, tm, D), lambda i: (k, cur(i), 0))
    const = lambda shape: pl.BlockSpec(shape, lambda i: (0,) * len(shape))
    return pl.pallas_call(
        _merge_kernel,
        grid=(nt + 1,),
        in_specs=[
            tile(), tile(), blk(4), blk(5), tile(),
            pl.BlockSpec((None, 6, D), lambda i: (cur(i) // spb, 0, 0)),
            const((D, D)), const((D, D)), const((D, D)), const((1, D)), const((D, 2 * LANES)), const((1, LANES)),
        ],
        out_specs=[tile(), pl.BlockSpec((tm, D // 2), lambda i: (cur(i), 0)),
                   pl.BlockSpec((tm, LANES), lambda i: (prev(i), 0)),
                   pl.BlockSpec((SUBLANES, tm), lambda i: (0, prev(i))), const((1, LANES))],
        out_shape=[jax.ShapeDtypeStruct((T, D), F32), jax.ShapeDtypeStruct((T, D // 2), U32),
                   jax.ShapeDtypeStruct((T, LANES), F32), jax.ShapeDtypeStruct((SUBLANES, T), F32),
                   jax.ShapeDtypeStruct((1, LANES), F32)],
        scratch_shapes=[pltpu.VMEM((1, LANES), F32), pltpu.VMEM((tm, LANES), F32)],
        compiler_params=_params("arbitrary"),
        name="merge",
    )(y_rg, y_dn, proj, proj, x2, mod3, wbr, wbd, wo, nw, wr, bias)


def _route_tile(lg, carry, live):
    tm = lg.shape[0]
    lane = lax.broadcasted_iota(I32, (tm, LANES), 1)
    big = jnp.int32(LANES)
    ninf = -jnp.inf

    def first_max(vals):
        vmax = jnp.max(vals, axis=-1, keepdims=True)
        return vmax, jnp.min(jnp.where(vals == vmax, lane, big), axis=-1, keepdims=True)

    is_g = lane < N_GROUPS
    gmax, gsel = first_max(jnp.where(is_g, lg, ninf))
    p_g = 1.0 / jnp.sum(jnp.where(is_g, jnp.exp(lg - gmax), 0.0), axis=-1, keepdims=True)
    e_lane = lane - N_GROUPS
    in_group = (e_lane >= 0) & (e_lane < N_EXPERTS) & ((e_lane >> 3) == gsel)
    el = jnp.where(in_group, lg, ninf)
    v1, l1 = first_max(el)
    v2, l2 = first_max(jnp.where(lane == l1, ninf, el))
    ex = jnp.exp(v2 - v1)
    w1 = p_g / (1.0 + ex)
    w2 = p_g * ex / (1.0 + ex)
    e1 = l1 - N_GROUPS
    e2 = l2 - N_GROUPS

    hit1 = lane == e1
    hit2 = lane == e2
    onehot = jnp.where(hit1 | hit2, 1.0, 0.0)
    ri = lax.broadcasted_iota(I32, (tm, tm), 0)
    ci = lax.broadcasted_iota(I32, (tm, tm), 1)
    before = jnp.where(ri > ci, 1.0, 0.0).astype(BF16)
    cum = jnp.dot(before, onehot.astype(BF16), preferred_element_type=F32) + carry[...]
    r1 = jnp.sum(jnp.where(hit1, cum, 0.0), axis=-1, keepdims=True)
    r2 = jnp.sum(jnp.where(hit2, cum, 0.0), axis=-1, keepdims=True)
    carry[...] = carry[...] + live * jnp.sum(onehot, axis=0, keepdims=True)

    slab = jnp.where(lane == 0, e1.astype(F32), 0.0)
    slab = jnp.where(lane == 1, e2.astype(F32), slab)
    slab = jnp.where(lane == 2, r1, slab)
    slab = jnp.where(lane == 3, r2, slab)
    slab = jnp.where(lane == 4, w1, slab)
    return jnp.where(lane == 5, w2, slab)


def _row_copy(src, dst, s, d, sem):
    return pltpu.make_async_copy(src.at[pl.ds(s, 1), :], dst.at[pl.ds(d, 1), :], sem)


def _sort_kernel(dest_ref, hp_ref, xs_in_ref, xs_ref, sem, *, tb):
    del xs_in_ref

    for r in range(tb):
        _row_copy(hp_ref, xs_ref, r, dest_ref[0, r], sem).start(priority=0)
        _row_copy(hp_ref, xs_ref, r, dest_ref[0, tb + r], sem).start(priority=1)

    def drain(r, carry):
        _row_copy(hp_ref, xs_ref, 0, 0, sem).wait()
        _row_copy(hp_ref, xs_ref, 0, 0, sem).wait()
        return carry

    lax.fori_loop(0, tb, drain, 0, unroll=8)


def _sort(dest3, hp, xs0, tb):
    T, Dh = hp.shape
    return pl.pallas_call(
        functools.partial(_sort_kernel, tb=tb),
        grid=(T // tb,),
        in_specs=[
            pl.BlockSpec((None, 1, 2 * tb), lambda i: (i, 0, 0), memory_space=pltpu.SMEM),
            pl.BlockSpec((tb, Dh), lambda i: (i, 0)),
            pl.BlockSpec(memory_space=pl.ANY),
        ],
        out_specs=pl.BlockSpec(memory_space=pl.ANY),
        out_shape=jax.ShapeDtypeStruct(xs0.shape, xs0.dtype),
        scratch_shapes=[pltpu.SemaphoreType.DMA],
        input_output_aliases={2: 0},
        compiler_params=pltpu.CompilerParams(dimension_semantics=("arbitrary",), has_side_effects=True),
        name="sort",
    )(dest3, hp, xs0)


def _experts_kernel(be_ref, nv_ref, x_ref, wg_ref, wu_ref, wd_ref, y_ref, wg_b, wu_b, wd_b):
    i = pl.program_id(0)
    valid = i < nv_ref[0]

    @pl.when(valid & ((i == 0) | (be_ref[i] != be_ref[jnp.maximum(i - 1, 0)])))
    def _():
        wg_b[...] = wg_ref[...].astype(BF16)
        wu_b[...] = wu_ref[...].astype(BF16)
        wd_b[...] = wd_ref[...].astype(BF16)

    @pl.when(valid)
    def _():
        x = _unpack_bf16_pairs(x_ref[...])
        g = jnp.dot(x, wg_b[...], preferred_element_type=F32)
        u = jnp.dot(x, wu_b[...], preferred_element_type=F32)
        y_ref[...] = jnp.dot((_silu(g) * u).astype(BF16), wd_b[...], preferred_element_type=F32)

    @pl.when(jnp.logical_not(valid))
    def _():
        y_ref[...] = jnp.zeros_like(y_ref)


def _experts(blk_e, nvb, xs, wg, wu, wd):
    P, Dh = xs.shape
    E, D, DE = wg.shape
    NB = P // MOE_BLOCK
    last = lambda i, nv: jnp.minimum(i, nv[0] - 1)
    grid_spec = pltpu.PrefetchScalarGridSpec(
        num_scalar_prefetch=2,
        grid=(NB,),
        in_specs=[
            pl.BlockSpec((MOE_BLOCK, Dh), lambda i, be, nv: (last(i, nv), 0)),
            pl.BlockSpec((None, D, DE), lambda i, be, nv: (be[last(i, nv)], 0, 0)),
            pl.BlockSpec((None, D, DE), lambda i, be, nv: (be[last(i, nv)], 0, 0)),
            pl.BlockSpec((None, DE, D), lambda i, be, nv: (be[last(i, nv)], 0, 0)),
        ],
        out_specs=pl.BlockSpec((MOE_BLOCK, D), lambda i, be, nv: (i, 0)),
        scratch_shapes=[pltpu.VMEM((D, DE), BF16), pltpu.VMEM((D, DE), BF16), pltpu.VMEM((DE, D), BF16)],
    )
    return pl.pallas_call(
        _experts_kernel,
        grid_spec=grid_spec,
        out_shape=jax.ShapeDtypeStruct((P, D), F32),
        compiler_params=_params("arbitrary"),
        name="experts",
    )(blk_e, nvb, xs, wg, wu, wd)


def _final_kernel(dest_ref, next_ref, x1_ref, slab_ref, mod_ref, fw_ref, ys_ref, o_ref, ybuf, sems, *, tb):
    i = pl.program_id(0)
    nt = pl.num_programs(0)

    def gather(idx_ref, slot):
        for r in range(tb):
            _row_copy(ys_ref, ybuf.at[slot, 0], idx_ref[0, r], r, sems.at[slot]).start(priority=0)
            _row_copy(ys_ref, ybuf.at[slot, 1], idx_ref[0, tb + r], r, sems.at[slot]).start(priority=1)

    def step(slot):
        @pl.when(i + 1 < nt)
        def _():
            gather(next_ref, 1 - slot)

        def drain(r, carry):
            _row_copy(ys_ref, ybuf.at[slot, 0], 0, 0, sems.at[slot]).wait()
            _row_copy(ys_ref, ybuf.at[slot, 1], 0, 0, sems.at[slot]).wait()
            return carry

        lax.fori_loop(0, tb, drain, 0, unroll=8)

        slab = slab_ref[...]
        y = slab[:, 4:5] * ybuf[slot, 0] + slab[:, 5:6] * ybuf[slot, 1]
        x2 = x1_ref[...] + mod_ref[5:6, :] * y
        o_ref[...] = x2 * lax.rsqrt(jnp.mean(x2 * x2, axis=-1, keepdims=True) + NORM_EPS) * fw_ref[...]

    @pl.when(i == 0)
    def _():
        gather(dest_ref, 0)

    for slot in range(2):
        pl.when(i % 2 == slot)(functools.partial(step, slot))


def _final(dest3, x1, slab, mod3, fw, ys, S, tb):
    T, D = x1.shape
    spb = S // tb
    nt = T // tb
    return pl.pallas_call(
        functools.partial(_final_kernel, tb=tb),
        grid=(nt,),
        in_specs=[
            pl.BlockSpec((None, 1, 2 * tb), lambda i: (i, 0, 0), memory_space=pltpu.SMEM),
            pl.BlockSpec((None, 1, 2 * tb), lambda i: (jnp.minimum(i + 1, nt - 1), 0, 0), memory_space=pltpu.SMEM),
            pl.BlockSpec((tb, D), lambda i: (i, 0)),
            pl.BlockSpec((tb, LANES), lambda i: (i, 0)),
            pl.BlockSpec((None, 6, D), lambda i: (i // spb, 0, 0)),
            pl.BlockSpec((1, D), lambda i: (0, 0)),
            pl.BlockSpec(memory_space=pl.ANY),
        ],
        out_specs=pl.BlockSpec((tb, D), lambda i: (i, 0)),
        out_shape=jax.ShapeDtypeStruct((T, D), F32),
        scratch_shapes=[pltpu.VMEM((2, 2, tb, D), F32), pltpu.SemaphoreType.DMA((2,))],
        compiler_params=_params("arbitrary"),
        name="final",
    )(dest3, dest3, x1, slab, mod3, fw, ys)


def _lane_row(vals, offset):
    return jnp.zeros((1, LANES), F32).at[0, offset:offset + vals.shape[0]].set(vals.astype(F32))


def _forward(x, c, w_ada, b_ada, norm1_w, w_in, rg_conv_w, rg_conv_b, rg_gate_a_w, rg_gate_a_b, rg_gate_x_w, rg_gate_x_b,
             rg_lambda, dn_conv_w, dn_a_log, dn_dt_bias, dn_norm_w, w_branch_rg, w_branch_dn, w_out, norm2_w,
             moe_w_group, moe_b_group, moe_w_router, moe_b_router, moe_w_gate, moe_w_up, moe_w_down, final_norm_w,
             tm_proj=512, tm_delta=256, tm_tok=512):
    B, S, D = x.shape
    T = B * S
    H = DN_HEADS
    x2 = x.reshape(T, D)
    mod3 = _ada(c, w_ada[0], b_ada[0]).reshape(B, 6, D)

    w = w_in[0]
    n_wide = 6 * D
    w_main = jnp.concatenate([w[:, :n_wide], w[:, n_wide + 2 * H:]], axis=1).astype(BF16)
    w_blocks = w_main.reshape(D, -1, D).transpose(1, 0, 2)
    w_small = jnp.pad(w[:, n_wide:n_wide + 2 * H], ((0, 0), (0, LANES - 2 * H))).astype(BF16)
    proj, ab, abt, y_rg = _inproj_rg(x2, mod3, norm1_w, w_blocks, w_small, rg_conv_w[0], rg_conv_b[0],
                                rg_gate_a_w[0].astype(BF16), rg_gate_a_b.reshape(-1),
                                rg_gate_x_w[0].astype(BF16), rg_gate_x_b.reshape(-1), rg_lambda[0], S, tm_proj)

    zeros_h = jnp.zeros((H,), F32)
    col = lambda v: jnp.concatenate([zeros_h, v.astype(F32)]).reshape(2 * H, 1)
    y_dn = _delta(proj, ab, abt, dn_conv_w[0], _lane_row(dn_a_log[0], H), _lane_row(dn_dt_bias[0], H),
                  col(dn_a_log[0]), col(dn_dt_bias[0]), dn_norm_w, B, S, tm_delta)

    w_route = jnp.pad(jnp.concatenate([moe_w_group[0], moe_w_router[0]], axis=1), ((0, 0), (0, LANES - N_GROUPS - N_EXPERTS)))
    w_route_hi = w_route.astype(BF16)
    w_route = jnp.concatenate([w_route_hi, (w_route - w_route_hi.astype(F32)).astype(BF16)], axis=1)
    bias = _lane_row(jnp.concatenate([moe_b_group[0], moe_b_router[0]]), 0)
    x1, hp, slab, slabt, cnt = _merge(y_rg, y_dn, proj, x2, mod3, w_branch_rg[0].astype(BF16),
                                      w_branch_dn[0].astype(BF16), w_out[0].astype(BF16), norm2_w, w_route, bias, S, tm_proj)

    eid = slabt[0:2].astype(I32)
    rank = slabt[2:4].astype(I32)
    counts = cnt[0, :N_EXPERTS].astype(I32)
    padded = (counts + MOE_BLOCK - 1) // MOE_BLOCK * MOE_BLOCK
    pad_end = jnp.cumsum(padded)
    pad_start = pad_end - padded
    seg = jnp.sum(jnp.where(eid[:, :, None] == jnp.arange(N_EXPERTS, dtype=I32), pad_start, 0), axis=-1)
    dest = seg + rank
    nt = T // tm_tok
    dest3 = dest.reshape(2, nt, tm_tok).transpose(1, 0, 2).reshape(nt, 1, 2 * tm_tok)
    A = 2 * T
    P = (A + N_EXPERTS * (MOE_BLOCK - 1) + MOE_BLOCK - 1) // MOE_BLOCK * MOE_BLOCK
    NB = P // MOE_BLOCK
    starts = jnp.arange(NB, dtype=I32) * MOE_BLOCK
    blk_e = jnp.minimum(jnp.sum(pad_end[None, :] <= starts[:, None], axis=1), N_EXPERTS - 1).astype(I32)
    nvb = (pad_end[-1:] // MOE_BLOCK).astype(I32)

    xs = _sort(dest3, hp, jnp.zeros((P, D // 2), U32), tm_tok)
    ys = _experts(blk_e, nvb, xs, moe_w_gate[0], moe_w_up[0], moe_w_down[0])
    out = _final(dest3, x1, slab, mod3, final_norm_w.reshape(1, D), ys, S, tm_tok)
    return out.reshape(B, S, D)


def kernel(x, c, w_ada, b_ada, norm1_w, w_in, rg_conv_w, rg_conv_b, rg_gate_a_w, rg_gate_a_b, rg_gate_x_w, rg_gate_x_b, rg_lambda, dn_conv_w, dn_a_log, dn_dt_bias, dn_norm_w, w_branch_rg, w_branch_dn, w_out, norm2_w, moe_w_group, moe_b_group, moe_w_router, moe_b_router, moe_w_gate, moe_w_up, moe_w_down, final_norm_w):
    return _forward(x, c, w_ada, b_ada, norm1_w, w_in, rg_conv_w, rg_conv_b, rg_gate_a_w, rg_gate_a_b, rg_gate_x_w, rg_gate_x_b, rg_lambda, dn_conv_w, dn_a_log, dn_dt_bias, dn_norm_w, w_branch_rg, w_branch_dn, w_out, norm2_w, moe_w_group, moe_b_group, moe_w_router, moe_b_router, moe_w_gate, moe_w_up, moe_w_down, final_norm_w)
```
